```python
import math
import jax, jax.numpy as jnp
from jax import lax
import numpy as np

D_MODEL = 1024
BATCH = 16
SEQ = 2048
DEPTH = 4

HEAD_DIM = 64
N_Q_HEADS = 16
N_KV_HEADS = 4
Q_PER_KV = N_Q_HEADS // N_KV_HEADS
WINDOW = 128
BLOCK = 128
ROPE_THETA = 10000.0
ATTN_Q_W = N_Q_HEADS * HEAD_DIM
ATTN_KV_W = N_KV_HEADS * HEAD_DIM
CONV_DIM = D_MODEL
CONV_WIDTH = 31
IN_COLS = ATTN_Q_W + 2 * ATTN_KV_W + 2 * CONV_DIM + 2 * D_MODEL
N_KEYS = 128
N_EXPERTS = N_KEYS * N_KEYS
N_RET_HEADS = 8
D_KEY = 256
HALF_KEY = D_KEY // 2
TOPK_HALF = 16
TOPK = 16
PEER_CHUNK = 128
EPS = 1e-6
NEG_INF = -1e30

kernel_name = "hybrid_swa_conformer_peer_adaln"


def rms_norm(x, g):
    xf = x.astype(jnp.float32)
    y = xf * lax.rsqrt(jnp.mean(xf * xf, axis=-1, keepdims=True) + EPS)
    return (y * g.astype(jnp.float32)).astype(x.dtype)


def layer_norm(x, g, b):
    xf = x.astype(jnp.float32)
    mu = jnp.mean(xf, axis=-1, keepdims=True)
    var = jnp.mean(jnp.square(xf - mu), axis=-1, keepdims=True)
    y = (xf - mu) * lax.rsqrt(var + EPS)
    return (y * g.astype(jnp.float32) + b.astype(jnp.float32)).astype(x.dtype)


def rope_tables(positions):
    inv = ROPE_THETA ** (-jnp.arange(0, HEAD_DIM, 2, dtype=jnp.float32) / HEAD_DIM)
    ang = positions.astype(jnp.float32)[..., None] * inv
    return jnp.cos(ang)[:, :, None, :], jnp.sin(ang)[:, :, None, :]


def apply_rope(t, cos, sin):
    t1, t2 = jnp.split(t.astype(jnp.float32), 2, axis=-1)
    return jnp.concatenate([t1 * cos - t2 * sin, t2 * cos + t1 * sin], axis=-1).astype(t.dtype)


def sliding_window_attention(q, k, v, sinks):
    B, S = q.shape[0], q.shape[1]
    nb = S // BLOCK
    qb = q.reshape(B, nb, BLOCK, N_KV_HEADS, Q_PER_KV, HEAD_DIM)

    def band(t):
        tb = t.reshape(B, nb, BLOCK, N_KV_HEADS, HEAD_DIM)
        prev = jnp.concatenate([jnp.zeros_like(tb[:, :1]), tb[:, :-1]], axis=1)
        return jnp.concatenate([prev, tb], axis=2)

    kb, vb = band(k), band(v)
    s = jnp.einsum('bnqgrd,bnkgd->bngrqk', qb, kb).astype(jnp.float32) * (HEAD_DIM ** -0.5)
    qi = jnp.arange(BLOCK)[:, None]
    kj = jnp.arange(2 * BLOCK)[None, :]
    diff = qi + BLOCK - kj
    in_window = (diff >= 0) & (diff < WINDOW)
    key_abs = jnp.arange(nb)[:, None, None] * BLOCK + kj[None] - BLOCK
    mask = in_window[None] & (key_abs >= 0)
    s = jnp.where(mask[None, :, None, None], s, NEG_INF)
    sink = sinks.astype(jnp.float32).reshape(N_KV_HEADS, Q_PER_KV)[None, None, :, :, None, None]
    m = jnp.maximum(jnp.max(s, axis=-1, keepdims=True), sink)
    p = jnp.exp(s - m)
    p = p / (jnp.sum(p, axis=-1, keepdims=True) + jnp.exp(sink - m))
    o = jnp.einsum('bngrqk,bnkgd->bnqgrd', p.astype(vb.dtype), vb)
    return o.reshape(B, S, N_Q_HEADS * HEAD_DIM)


def causal_depthwise_conv(u, w, b):
    y = lax.conv_general_dilated(u, w[:, None, :].astype(u.dtype), window_strides=(1,),
                                 padding=[(CONV_WIDTH - 1, 0)],
                                 dimension_numbers=('NWC', 'WIO', 'NWC'),
                                 feature_group_count=u.shape[-1])
    return y + b.astype(u.dtype)


def hybrid_mixer(h, cos, sin, w_in, q_norm_g, k_norm_g, sinks, w_o_attn,
                 conv_w, conv_b, conv_ln_g, conv_ln_b, w_o_conv, w_out):
    B, S, _ = h.shape
    z = h @ w_in
    cuts = [ATTN_Q_W, ATTN_Q_W + ATTN_KV_W, ATTN_Q_W + 2 * ATTN_KV_W,
            ATTN_Q_W + 2 * ATTN_KV_W + 2 * CONV_DIM]
    q, k, v, glu, gates = jnp.split(z, cuts, axis=-1)
    q = apply_rope(rms_norm(q.reshape(B, S, N_Q_HEADS, HEAD_DIM), q_norm_g), cos, sin)
    k = apply_rope(rms_norm(k.reshape(B, S, N_KV_HEADS, HEAD_DIM), k_norm_g), cos, sin)
    v = v.reshape(B, S, N_KV_HEADS, HEAD_DIM)
    attn = sliding_window_attention(q, k, v, sinks) @ w_o_attn
    ga, gb = jnp.split(glu, 2, axis=-1)
    u = ga * jax.nn.sigmoid(gb)
    u = causal_depthwise_conv(u, conv_w, conv_b)
    u = jax.nn.silu(layer_norm(u, conv_ln_g, conv_ln_b))
    conv = u @ w_o_conv
    g_attn, g_conv = jnp.split(gates, 2, axis=-1)
    merged = jax.nn.sigmoid(g_attn) * attn + jax.nn.sigmoid(g_conv) * conv
    return merged @ w_out


def peer(h, w_query, sub_keys, u_tab, v_tab):
    B, S, D = h.shape
    T = B * S
    xt = h.reshape(T, D)
    q = (xt @ w_query).reshape(T, N_RET_HEADS, 2, HALF_KEY)
    s = jnp.einsum('thpd,hpnd->thpn', q, sub_keys).astype(jnp.float32)
    s_half, i_half = lax.top_k(s, TOPK_HALF)
    cand_s = (s_half[:, :, 0, :, None] + s_half[:, :, 1, None, :]).reshape(T, N_RET_HEADS, TOPK_HALF * TOPK_HALF)
    cand_i = (i_half[:, :, 0, :, None] * N_KEYS + i_half[:, :, 1, None, :]).reshape(T, N_RET_HEADS, TOPK_HALF * TOPK_HALF)
    top_s, pos = lax.top_k(cand_s, TOPK)
    ids = jnp.take_along_axis(cand_i, pos, axis=-1)
    gates = jax.nn.softmax(top_s, axis=-1)

    def chunk(args):
        xc, idc, gc = args
        a = jax.nn.gelu(jnp.einsum('chkd,cd->chk', u_tab[idc], xc).astype(jnp.float32), approximate=False)
        coef = (gc * a).astype(xc.dtype)
        return jnp.einsum('chk,chkd->cd', coef, v_tab[idc])

    nc = T // PEER_CHUNK
    y = lax.map(chunk, (xt.reshape(nc, PEER_CHUNK, D),
                        ids.reshape(nc, PEER_CHUNK, N_RET_HEADS, TOPK),
                        gates.reshape(nc, PEER_CHUNK, N_RET_HEADS, TOPK)))
    return y.reshape(B, S, D)


def setup_inputs(seed: int = 0) -> dict:
    key = jax.random.key(seed)
    ks = jax.random.split(key, 24)
    f32 = jnp.float32
    D = D_MODEL

    def nrm(k, shape, scale):
        return jax.random.normal(k, shape, f32) * scale

    x = jax.random.normal(ks[0], (BATCH, SEQ, D), f32)
    c = jax.random.normal(ks[1], (BATCH, D), f32)
    offset = jax.random.randint(ks[2], (BATCH, 1), 0, 1024, dtype=jnp.int32)
    positions = (jnp.arange(SEQ, dtype=jnp.int32)[None, :] + offset).astype(jnp.int32)
    return {
        "x": x,
        "c": c,
        "positions": positions,
        "ada_w": nrm(ks[3], (DEPTH, D, 6 * D), 0.5 * D ** -0.5),
        "ada_b": nrm(ks[4], (DEPTH, 6 * D), 0.01),
        "norm_mix_g": 1.0 + nrm(ks[5], (DEPTH, D), 0.05),
        "w_in": nrm(ks[6], (DEPTH, D, IN_COLS), D ** -0.5),
        "q_norm_g": 1.0 + nrm(ks[7], (DEPTH, HEAD_DIM), 0.05),
        "k_norm_g": 1.0 + nrm(ks[8], (DEPTH, HEAD_DIM), 0.05),
        "attn_sinks": nrm(ks[9], (DEPTH, N_Q_HEADS), 0.5),
        "w_o_attn": nrm(ks[10], (DEPTH, ATTN_Q_W, D), ATTN_Q_W ** -0.5),
        "conv_w": nrm(ks[11], (DEPTH, CONV_WIDTH, CONV_DIM), CONV_WIDTH ** -0.5),
        "conv_b": nrm(ks[12], (DEPTH, CONV_DIM), 0.01),
        "conv_ln_g": 1.0 + nrm(ks[13], (DEPTH, CONV_DIM), 0.05),
        "conv_ln_b": nrm(ks[14], (DEPTH, CONV_DIM), 0.01),
        "w_o_conv": nrm(ks[15], (DEPTH, CONV_DIM, D), CONV_DIM ** -0.5),
        "w_out": nrm(ks[16], (DEPTH, D, D), D ** -0.5),
        "norm_ffn_g": 1.0 + nrm(ks[17], (DEPTH, D), 0.05),
        "peer_w_query": nrm(ks[18], (DEPTH, D, N_RET_HEADS * D_KEY), D ** -0.5),
        "peer_sub_keys": nrm(ks[19], (DEPTH, N_RET_HEADS, 2, N_KEYS, HALF_KEY), HALF_KEY ** -0.5),
        "peer_u": nrm(ks[20], (DEPTH, N_EXPERTS, D), D ** -0.5),
        "peer_v": nrm(ks[21], (DEPTH, N_EXPERTS, D), 0.5),
    }


def reference(x, c, positions, ada_w, ada_b, norm_mix_g, w_in, q_norm_g, k_norm_g, attn_sinks,
              w_o_attn, conv_w, conv_b, conv_ln_g, conv_ln_b, w_o_conv, w_out, norm_ffn_g,
              peer_w_query, peer_sub_keys, peer_u, peer_v):
    cos, sin = rope_tables(positions)
    c_act = jax.nn.silu(c)
    for l in range(DEPTH):
        mod = c_act @ ada_w[l] + ada_b[l]
        sh1, sc1, g1, sh2, sc2, g2 = [m[:, None, :] for m in jnp.split(mod, 6, axis=-1)]
        h = rms_norm(x, norm_mix_g[l]) * (1.0 + sc1) + sh1
        x = x + g1 * hybrid_mixer(h, cos, sin, w_in[l], q_norm_g[l], k_norm_g[l], attn_sinks[l],
                                  w_o_attn[l], conv_w[l], conv_b[l], conv_ln_g[l], conv_ln_b[l],
                                  w_o_conv[l], w_out[l])
        h = rms_norm(x, norm_ffn_g[l]) * (1.0 + sc2) + sh2
        x = x + g2 * peer(h, peer_w_query[l], peer_sub_keys[l], peer_u[l], peer_v[l])
    return x
```

```python
import numpy as np
import jax
import jax.numpy as jnp
from jax import lax
from jax.experimental import pallas as pl
from jax.experimental.pallas import tpu as pltpu

F32 = jnp.float32
BF16 = jnp.bfloat16

HEAD_DIM = 64
N_Q_HEADS = 16
N_KV_HEADS = 4
Q_PER_KV = N_Q_HEADS // N_KV_HEADS
WINDOW = 128
ROPE_THETA = 10000.0
CONV_WIDTH = 31
N_KEYS = 128
N_RET_HEADS = 8
TOPK = 16
N_PICKS = N_RET_HEADS * TOPK
EPS = 1e-6
NEG_INF = -1e30

LANES = 128
PAIR_ROWS = 16
CHUNKS = 8
PICKS_PER_TILE = 32
N_TILES = N_PICKS // PICKS_PER_TILE
HALF_TILE = PICKS_PER_TILE // 2


def _cparams(vmem_mb, sem):
    return pltpu.CompilerParams(dimension_semantics=sem, vmem_limit_bytes=vmem_mb * 1024 * 1024)


def _resident(block_shape, index_map):
    return pl.BlockSpec(block_shape, index_map, pipeline_mode=pl.Buffered(1))


def _split_bf16(v):
    hi = v.astype(BF16)
    lo = (v - hi.astype(F32)).astype(BF16)
    return hi, lo


def _dot(a, b):
    return jnp.dot(a, b, preferred_element_type=F32)


def _dot_nt(a, b):
    return lax.dot_general(a, b, (((1,), (1,)), ((), ())), preferred_element_type=F32)


def _rope_kernel(pos_ref, inv_ref, cs_ref):
    ang = inv_ref[...] * pos_ref[...].astype(F32)
    cs_ref[0:32, :] = jnp.cos(ang)
    cs_ref[32:64, :] = jnp.sin(ang)


def _rope_tables(positions):
    T = positions.size
    tb = min(2048, T)
    inv = ROPE_THETA ** (-jnp.arange(0, HEAD_DIM, 2, dtype=F32) / HEAD_DIM)
    cs = pl.pallas_call(
        _rope_kernel,
        grid=(T // tb,),
        in_specs=[pl.BlockSpec((1, tb), lambda i: (0, i)), pl.BlockSpec((32, 1), lambda i: (0, 0))],
        out_specs=pl.BlockSpec((64, tb), lambda i: (0, i)),
        out_shape=jax.ShapeDtypeStruct((64, T), F32),
        compiler_params=_cparams(32, ("arbitrary",)),
        name="rope_tables",
    )(positions.reshape(1, T), inv.reshape(32, 1))
    cos = cs[0:32].T
    sin = cs[32:64].T
    cos128 = jnp.concatenate([cos, cos, cos, cos], axis=1)
    sin128 = jnp.concatenate([-sin, sin, -sin, sin], axis=1)
    return cos128, sin128


def _mod_kernel(c_ref, w_ref, b_ref, o_ref):
    c = c_ref[...]
    ca = c * jax.nn.sigmoid(c)
    o_ref[...] = jnp.dot(ca, w_ref[...], preferred_element_type=F32,
                         precision=lax.Precision.HIGHEST) + b_ref[...]


def _modulation(c, ada_w, ada_b):
    depth, D, six_d = ada_w.shape
    B = c.shape[0]
    tn = 1024
    return pl.pallas_call(
        _mod_kernel,
        grid=(depth, six_d // tn),
        in_specs=[
            pl.BlockSpec((B, D), lambda l, j: (0, 0)),
            pl.BlockSpec((None, D, tn), lambda l, j: (l, 0, j)),
            pl.BlockSpec((None, 1, tn), lambda l, j: (l, 0, j)),
        ],
        out_specs=pl.BlockSpec((None, B, tn), lambda l, j: (l, 0, j)),
        out_shape=jax.ShapeDtypeStruct((depth, B, six_d), F32),
        compiler_params=_cparams(32, ("arbitrary", "arbitrary")),
        name="adaln_modulation",
    )(c, ada_w, ada_b.reshape(depth, 1, six_d))


def _in_kernel(x_ref, mod_ref, g_ref, w_ref, q_ref, kv_ref, glu_ref, gates_ref):
    D = x_ref.shape[1]
    x = x_ref[...]
    y = x * lax.rsqrt(jnp.mean(x * x, axis=-1, keepdims=True) + EPS) * g_ref[...]
    h = (y * (1.0 + mod_ref[:, D:2 * D]) + mod_ref[:, 0:D]).astype(BF16)
    c0 = 0
    for ref in (q_ref, kv_ref, glu_ref, gates_ref):
        n = ref.shape[1]
        ref[...] = _dot(h, w_ref[:, c0:c0 + n])
        c0 += n


def _in_projection(xt, mod4, norm_g, w_in_bf, l, S):
    T, D = xt.shape
    n_in = w_in_bf.shape[2]
    tm = 256
    q_w = N_Q_HEADS * HEAD_DIM
    kv_w = 2 * N_KV_HEADS * HEAD_DIM
    rest = (n_in - q_w - kv_w) // 2
    spb = S // tm
    widths = (q_w, kv_w, rest, rest)
    return pl.pallas_call(
        _in_kernel,
        grid=(T // tm,),
        in_specs=[
            pl.BlockSpec((tm, D), lambda i: (i, 0)),
            pl.BlockSpec((None, None, 1, mod4.shape[3]), lambda i: (l, i // spb, 0, 0)),
            pl.BlockSpec((None, 1, D), lambda i: (l, 0, 0)),
            _resident((None, D, n_in), lambda i: (l, 0, 0)),
        ],
        out_specs=[pl.BlockSpec((tm, w), lambda i: (i, 0)) for w in widths],
        out_shape=[jax.ShapeDtypeStruct((T, w), F32) for w in widths],
        compiler_params=_cparams(48, ("arbitrary",)),
        name="in_projection",
    )(xt, mod4, norm_g, w_in_bf)


def _norm_rope_chunk(raw, g, cos, sin, lane):
    sq = raw * raw
    low = lane < HEAD_DIM
    ss0 = jnp.sum(jnp.where(low, sq, 0.0), axis=-1, keepdims=True)
    ss1 = jnp.sum(jnp.where(low, 0.0, sq), axis=-1, keepdims=True)
    scale = jnp.where(low, lax.rsqrt(ss0 / HEAD_DIM + EPS), lax.rsqrt(ss1 / HEAD_DIM + EPS))
    xg = raw * scale * g
    first = (lane % HEAD_DIM) < (HEAD_DIM // 2)
    rot = jnp.where(first, pltpu.roll(xg, LANES - HEAD_DIM // 2, 1), pltpu.roll(xg, HEAD_DIM // 2, 1))
    return xg * cos + rot * sin


def _attn_kernel(sink_ref, q_ref, kvc_ref, kvp_ref, cosc_ref, sinc_ref, cosp_ref, sinp_ref,
                 gq_ref, gk_ref, o_ref, kd_scr, vd_scr):
    tq = q_ref.shape[0]
    blk = WINDOW
    j = pl.program_id(1)
    kvw = N_KV_HEADS * HEAD_DIM
    gq = gq_ref[...]
    gk = gk_ref[...]

    for (src, cos_r, sin_r, r0, n) in ((kvp_ref, cosp_ref, sinp_ref, 0, blk),
                                       (kvc_ref, cosc_ref, sinc_ref, blk, tq)):
        lane = lax.broadcasted_iota(jnp.int32, (n, LANES), 1)
        low = lane < HEAD_DIM
        cos = cos_r[...]
        sin = sin_r[...]
        for c in range(kvw // LANES):
            kc = _norm_rope_chunk(src[:, c * LANES:(c + 1) * LANES], gk, cos, sin, lane)
            vc = src[:, kvw + c * LANES:kvw + (c + 1) * LANES]
            kc_sw = pltpu.roll(kc, HEAD_DIM, 1)
            vc_sw = pltpu.roll(vc, HEAD_DIM, 1)
            kd_scr[2 * c, r0:r0 + n, :] = jnp.where(low, kc, kc_sw).astype(BF16)
            kd_scr[2 * c + 1, r0:r0 + n, :] = jnp.where(low, kc_sw, kc).astype(BF16)
            vd_scr[2 * c, r0:r0 + n, :] = jnp.where(low, vc, vc_sw).astype(BF16)
            vd_scr[2 * c + 1, r0:r0 + n, :] = jnp.where(low, vc_sw, vc).astype(BF16)

    lane_q = lax.broadcasted_iota(jnp.int32, (blk, LANES), 1)
    low_q = lane_q < HEAD_DIM
    qi = lax.broadcasted_iota(jnp.int32, (blk, 2 * blk), 0)
    kj = lax.broadcasted_iota(jnp.int32, (blk, 2 * blk), 1)
    diff = qi + blk - kj
    in_window = (diff >= 0) & (diff < WINDOW)
    scale = HEAD_DIM ** -0.5

    for n in range(tq // blk):
        rows = slice(n * blk, (n + 1) * blk)
        cos = cosc_ref[rows, :]
        sin = sinc_ref[rows, :]
        valid = in_window
        if n == 0:
            valid = valid & (kj >= jnp.where(j > 0, 0, blk))
        valid4 = jnp.concatenate([valid] * Q_PER_KV, axis=0)
        for g in range(N_KV_HEADS):
            qm = []
            for c in range(2 * g, 2 * g + 2):
                qc = _norm_rope_chunk(q_ref[rows, c * LANES:(c + 1) * LANES], gq, cos, sin, lane_q) * scale
                qm.append(jnp.where(low_q, qc, 0.0).astype(BF16))
                qm.append(jnp.where(low_q, 0.0, qc).astype(BF16))
            q4 = jnp.concatenate(qm, axis=0)
            kd = kd_scr[g, n * blk:(n + 2) * blk, :]
            vd = vd_scr[g, n * blk:(n + 2) * blk, :]
            s = _dot_nt(q4, kd)
            s = jnp.where(valid4, s, NEG_INF)
            sink = jnp.concatenate(
                [jnp.full((blk, 1), sink_ref[g * Q_PER_KV + r], F32) for r in range(Q_PER_KV)], axis=0)
            m = jnp.maximum(jnp.max(s, axis=-1, keepdims=True), sink)
            p = jnp.exp(s - m)
            p = p / (jnp.sum(p, axis=-1, keepdims=True) + jnp.exp(sink - m))
            o4 = _dot(p.astype(BF16), vd)
            for r2 in range(Q_PER_KV // 2):
                c = (g * Q_PER_KV) // 2 + r2
                oe = o4[(2 * r2) * blk:(2 * r2 + 1) * blk, :]
                oo = o4[(2 * r2 + 1) * blk:(2 * r2 + 2) * blk, :]
                o_ref[rows, c * LANES:(c + 1) * LANES] = jnp.where(low_q, oe, oo).astype(BF16)


def _attention(q, kv, cos128, sin128, gq, gk, sinks, B, S):
    T = q.shape[0]
    tq = 512
    blk = WINDOW
    nq = S // tq
    bpq = tq // blk
    cur = lambda b, j: (b * nq + j, 0)
    prev = lambda b, j: (jnp.maximum((b * nq + j) * bpq - 1, 0), 0)
    kvw = kv.shape[1]
    return pl.pallas_call(
        _attn_kernel,
        grid=(B, nq),
        in_specs=[
            pl.BlockSpec(memory_space=pltpu.SMEM),
            pl.BlockSpec((tq, q.shape[1]), cur),
            pl.BlockSpec((tq, kvw), cur),
            pl.BlockSpec((blk, kvw), prev),
            pl.BlockSpec((tq, LANES), cur),
            pl.BlockSpec((tq, LANES), cur),
            pl.BlockSpec((blk, LANES), prev),
            pl.BlockSpec((blk, LANES), prev),
            pl.BlockSpec((1, LANES), lambda b, j: (0, 0)),
            pl.BlockSpec((1, LANES), lambda b, j: (0, 0)),
        ],
        out_specs=pl.BlockSpec((tq, q.shape[1]), cur),
        scratch_shapes=[pltpu.VMEM((N_KV_HEADS, tq + blk, LANES), BF16),
                        pltpu.VMEM((N_KV_HEADS, tq + blk, LANES), BF16)],
        out_shape=jax.ShapeDtypeStruct((T, q.shape[1]), BF16),
        compiler_params=_cparams(48, ("arbitrary", "arbitrary")),
        name="swa_attention",
    )(sinks, q, kv, kv, cos128, sin128, cos128, sin128, gq, gk)


HALO = 32


def _conv_kernel(glu_ref, cw_ref, cb_ref, lg_ref, lb_ref, wo_ref, o_ref, ubuf):
    tc = glu_ref.shape[0]
    C = o_ref.shape[1]
    j = pl.program_id(1)

    @pl.when(j == 0)
    def _():
        ubuf[0:HALO, :] = jnp.zeros((HALO, C), F32)

    @pl.when(j > 0)
    def _():
        ubuf[0:HALO, :] = ubuf[tc:tc + HALO, :]

    ubuf[HALO:HALO + tc, :] = glu_ref[:, 0:C] * jax.nn.sigmoid(glu_ref[:, C:2 * C])
    off = HALO - (CONV_WIDTH - 1)
    acc = cb_ref[...] + cw_ref[0:1, :] * ubuf[off:off + tc, :]
    for k in range(1, CONV_WIDTH):
        acc = acc + cw_ref[k:k + 1, :] * ubuf[off + k:off + k + tc, :]
    mu = jnp.mean(acc, axis=-1, keepdims=True)
    d = acc - mu
    var = jnp.mean(d * d, axis=-1, keepdims=True)
    yn = d * lax.rsqrt(var + EPS) * lg_ref[...] + lb_ref[...]
    act = yn * jax.nn.sigmoid(yn)
    o_ref[...] = _dot(act.astype(BF16), wo_ref[...])


def _conv_branch(glu, conv_w, conv_b, ln_g, ln_b, w_o_conv_bf, l, B, S):
    T = glu.shape[0]
    C = glu.shape[1] // 2
    tc = 256
    nc = S // tc
    vec = lambda: pl.BlockSpec((None, 1, C), lambda b, j: (l, 0, 0))
    return pl.pallas_call(
        _conv_kernel,
        grid=(B, nc),
        in_specs=[
            pl.BlockSpec((tc, 2 * C), lambda b, j: (b * nc + j, 0)),
            pl.BlockSpec((None, CONV_WIDTH, C), lambda b, j: (l, 0, 0)),
            vec(), vec(), vec(),
            _resident((None, C, C), lambda b, j: (l, 0, 0)),
        ],
        out_specs=pl.BlockSpec((tc, C), lambda b, j: (b * nc + j, 0)),
        out_shape=jax.ShapeDtypeStruct((T, C), F32),
        scratch_shapes=[pltpu.VMEM((tc + HALO, C), F32)],
        compiler_params=_cparams(48, ("arbitrary", "arbitrary")),
        name="conv_branch",
    )(glu, conv_w, conv_b, ln_g, ln_b, w_o_conv_bf)


def _merge_kernel(x_ref, ao_ref, co_ref, gates_ref, woa_ref, wout_ref, mod_ref, gf_ref,
                  xmid_ref, h2_ref):
    D = x_ref.shape[1]
    attn = _dot(ao_ref[...], woa_ref[...])
    merged = (jax.nn.sigmoid(gates_ref[:, 0:D]) * attn
              + jax.nn.sigmoid(gates_ref[:, D:2 * D]) * co_ref[...])
    out = _dot(merged.astype(BF16), wout_ref[...])
    xm = x_ref[...] + mod_ref[:, 2 * D:3 * D] * out
    xmid_ref[...] = xm
    y = xm * lax.rsqrt(jnp.mean(xm * xm, axis=-1, keepdims=True) + EPS) * gf_ref[...]
    h2_ref[...] = y * (1.0 + mod_ref[:, 4 * D:5 * D]) + mod_ref[:, 3 * D:4 * D]


def _merge(xt, attn_o, conv_o, gates, w_o_attn_bf, w_out_bf, mod4, norm_ffn_g, l, S):
    T, D = xt.shape
    tm = 256
    spb = S // tm
    row = lambda w: pl.BlockSpec((tm, w), lambda i: (i, 0))
    return pl.pallas_call(
        _merge_kernel,
        grid=(T // tm,),
        in_specs=[
            row(D), row(D), row(D), row(2 * D),
            _resident((None, D, D), lambda i: (l, 0, 0)),
            _resident((None, D, D), lambda i: (l, 0, 0)),
            pl.BlockSpec((None, None, 1, mod4.shape[3]), lambda i: (l, i // spb, 0, 0)),
            pl.BlockSpec((None, 1, D), lambda i: (l, 0, 0)),
        ],
        out_specs=[row(D), row(D)],
        out_shape=[jax.ShapeDtypeStruct((T, D), F32), jax.ShapeDtypeStruct((T, D), F32)],
        compiler_params=_cparams(48, ("arbitrary",)),
        name="merge_out_projection",
    )(xt, attn_o, conv_o, gates, w_o_attn_bf, w_out_bf, mod4, norm_ffn_g)


def _topk_rows(s, payload, k):
    R, n = s.shape
    rid = lax.broadcasted_iota(jnp.int32, (R, n), 0).astype(F32)
    vals, pays = [], []
    for _ in range(k):
        m = jnp.max(s, axis=0, keepdims=True)
        pos = jnp.min(jnp.where(s == m, rid, float(R)), axis=0, keepdims=True)
        sel = rid == pos
        if payload is None:
            pays.append(pos)
        else:
            pays.append(jnp.max(jnp.where(sel, payload, -1.0), axis=0, keepdims=True))
        vals.append(m)
        s = jnp.where(sel, -jnp.inf, s)
    return jnp.concatenate(vals, axis=0), jnp.concatenate(pays, axis=0)


def _route_kernel(h_ref, wq_ref, keys_ref, pair_ref, par_ref, gate_ref):
    q = _dot(h_ref[...].astype(BF16), wq_ref[...])
    ids_all, gates_all = [], []
    for h in range(N_RET_HEADS):
        half = []
        for p in range(2):
            hp = 2 * h + p
            qs = q[:, hp * N_KEYS:(hp + 1) * N_KEYS].astype(BF16)
            st = _dot_nt(keys_ref[hp], qs)
            half.append(_topk_rows(st, None, TOPK))
        (v0, i0), (v1, i1) = half
        cand_s = jnp.concatenate([v0[a:a + 1, :] + v1 for a in range(TOPK)], axis=0)
        cand_i = jnp.concatenate([i0[a:a + 1, :] * float(N_KEYS) + i1 for a in range(TOPK)], axis=0)
        top_s, top_i = _topk_rows(cand_s, cand_i, TOPK)
        e = jnp.exp(top_s - top_s[0:1, :])
        gates_all.append(e / jnp.sum(e, axis=0, keepdims=True))
        ids_all.append(top_i)
    ids = jnp.concatenate(ids_all, axis=0).T.astype(jnp.int32)
    gate_ref[...] = jnp.concatenate(gates_all, axis=0).T
    pair_ref[...] = lax.shift_right_logical(ids, 1)
    par_ref[...] = ids & 1


def _route(h2, w_query_bf, keys_bf, l):
    T, D = h2.shape
    tr = 128
    nq = w_query_bf.shape[2]
    out = lambda dt: jax.ShapeDtypeStruct((T, N_PICKS), dt)
    return pl.pallas_call(
        _route_kernel,
        grid=(T // tr,),
        in_specs=[
            pl.BlockSpec((tr, D), lambda i: (i, 0)),
            _resident((None, D, nq), lambda i: (l, 0, 0)),
            _resident((None, 2 * N_RET_HEADS, N_KEYS, N_KEYS), lambda i: (l, 0, 0, 0)),
        ],
        out_specs=[pl.BlockSpec((tr, N_PICKS), lambda i: (i, 0))] * 3,
        out_shape=[out(jnp.int32), out(jnp.int32), out(F32)],
        compiler_params=_cparams(32, ("arbitrary",)),
        name="peer_route",
    )(h2, w_query_bf, keys_bf)


def _tile_weights(tab_ref, pair_ref, t, tile):
    base = tile * PICKS_PER_TILE
    wa = jnp.concatenate([tab_ref[pair_ref[t, base + i]] for i in range(HALF_TILE)], axis=0)
    wb = jnp.concatenate([tab_ref[pair_ref[t, base + HALF_TILE + i]] for i in range(HALF_TILE)], axis=0)
    return jnp.concatenate([wa, wb], axis=1)


def _peer1_kernel(pair_ref, par_ref, gate_ref, h_ref, tab_ref, ge_ref, go_ref, coef_ref, u_scr):
    tq = h_ref.shape[0]
    tile_w = 2 * HALF_TILE * PAIR_ROWS
    half_w = HALF_TILE * PAIR_ROWS
    sub = lax.broadcasted_iota(jnp.int32, (CHUNKS, half_w), 0)
    lane = lax.broadcasted_iota(jnp.int32, (CHUNKS, half_w), 1)
    diag = (lane % CHUNKS) == sub
    zero = jnp.zeros((CHUNKS, LANES), BF16)

    def token(t, carry):
        xhi, xlo = _split_bf16(h_ref[t])
        lhs = jnp.concatenate([
            jnp.concatenate([xhi, zero], axis=1), jnp.concatenate([xlo, zero], axis=1),
            jnp.concatenate([zero, xhi], axis=1), jnp.concatenate([zero, xlo], axis=1)], axis=0)
        for tile in range(N_TILES):
            s = _dot_nt(lhs, _tile_weights(tab_ref, pair_ref, t, tile))
            sa = s[0:CHUNKS] + s[CHUNKS:2 * CHUNKS]
            sb = s[2 * CHUNKS:3 * CHUNKS] + s[3 * CHUNKS:4 * CHUNKS]
            ua = jnp.sum(jnp.where(diag, sa, 0.0), axis=0, keepdims=True)
            ub = jnp.sum(jnp.where(diag, sb, 0.0), axis=0, keepdims=True)
            u_scr[pl.ds(t, 1), tile * tile_w:tile * tile_w + half_w] = ua
            u_scr[pl.ds(t, 1), tile * tile_w + half_w:(tile + 1) * tile_w] = ub
        return carry

    lax.fori_loop(0, tq, token, 0)
    uhi, ulo = _split_bf16(u_scr[...])
    a_even = _dot(uhi, ge_ref[...]) + _dot(ulo, ge_ref[...])
    a_odd = _dot(uhi, go_ref[...]) + _dot(ulo, go_ref[...])
    a = jnp.where(par_ref[...] == 1, a_odd, a_even)
    gelu = 0.5 * a * (1.0 + lax.erf(a * (2.0 ** -0.5)))
    coef_ref[...] = gate_ref[...] * gelu


def _peer2_kernel(pair_ref, par_ref, coef_ref, x_ref, g2_ref, tab_ref, ee_ref, eo_ref, o_ref, c_scr):
    tq = x_ref.shape[0]
    half_w = HALF_TILE * PAIR_ROWS
    tile_w = 2 * half_w
    c = coef_ref[...]
    odd = par_ref[...] == 1
    cehi, celo = _split_bf16(jnp.where(odd, 0.0, c))
    cohi, colo = _split_bf16(jnp.where(odd, c, 0.0))
    c_scr[...] = (_dot(cehi, ee_ref[...]) + _dot(celo, ee_ref[...])
                  + _dot(cohi, eo_ref[...]) + _dot(colo, eo_ref[...]))
    sub = lax.broadcasted_iota(jnp.int32, (PAIR_ROWS, half_w), 0)
    lane = lax.broadcasted_iota(jnp.int32, (PAIR_ROWS, half_w), 1)
    diag = (lane % PAIR_ROWS) == sub
    g2 = g2_ref[...]

    def token(t, carry):
        acc = jnp.zeros((4 * PAIR_ROWS, 2 * LANES), F32)
        for tile in range(N_TILES):
            parts = []
            for half in range(2):
                r = c_scr[pl.ds(t, 1), tile * tile_w + half * half_w:tile * tile_w + (half + 1) * half_w]
                lmat = jnp.where(diag, jnp.broadcast_to(r, (PAIR_ROWS, half_w)), 0.0)
                parts.extend(_split_bf16(lmat))
            lhs = jnp.concatenate(parts, axis=0)
            acc = acc + _dot(lhs, _tile_weights(tab_ref, pair_ref, t, tile))
        y16 = (acc[0:PAIR_ROWS, 0:LANES] + acc[PAIR_ROWS:2 * PAIR_ROWS, 0:LANES]
               + acc[2 * PAIR_ROWS:3 * PAIR_ROWS, LANES:] + acc[3 * PAIR_ROWS:, LANES:])
        y = y16[0:CHUNKS] + y16[CHUNKS:]
        o_ref[t] = x_ref[t] + g2 * y
        return carry

    lax.fori_loop(0, tq, token, 0)


def _peer_constants():
    n = N_PICKS * PAIR_ROWS
    col = np.arange(n)
    pick = col // PAIR_ROWS
    upper = (col % PAIR_ROWS) >= CHUNKS
    sel = pick[:, None] == np.arange(N_PICKS)[None, :]
    ge = (sel & ~upper[:, None]).astype(np.float32)
    go = (sel & upper[:, None]).astype(np.float32)
    as_bf = lambda a: jnp.asarray(a, dtype=BF16)
    return as_bf(ge), as_bf(go), as_bf(ge.T), as_bf(go.T)


def _peer_tokens_per_step():
    return 256


def _peer1(pair, par, gate, h3, tab, ge, go, l):
    T = pair.shape[0]
    tq = _peer_tokens_per_step()
    n = N_PICKS * PAIR_ROWS
    row = lambda: pl.BlockSpec((tq, N_PICKS), lambda i: (i, 0))
    return pl.pallas_call(
        _peer1_kernel,
        grid=(T // tq,),
        in_specs=[
            pl.BlockSpec((tq, N_PICKS), lambda i: (i, 0), memory_space=pltpu.SMEM),
            row(), row(),
            pl.BlockSpec((tq, CHUNKS, LANES), lambda i: (i, 0, 0)),
            _resident((None,) + tab.shape[1:], lambda i: (l, 0, 0, 0)),
            _resident((n, N_PICKS), lambda i: (0, 0)),
            _resident((n, N_PICKS), lambda i: (0, 0)),
        ],
        out_specs=row(),
        out_shape=jax.ShapeDtypeStruct((T, N_PICKS), F32),
        scratch_shapes=[pltpu.VMEM((tq, n), F32)],
        compiler_params=_cparams(48, ("arbitrary",)),
        name="peer_scores",
    )(pair, par, gate, h3, tab, ge, go)


def _peer2(pair, par, coef, x3, g2, tab, ee, eo, l, S):
    T = pair.shape[0]
    tq = _peer_tokens_per_step()
    n = N_PICKS * PAIR_ROWS
    spb = S // tq
    row = lambda: pl.BlockSpec((tq, N_PICKS), lambda i: (i, 0))
    tok = lambda: pl.BlockSpec((tq, CHUNKS, LANES), lambda i: (i, 0, 0))
    return pl.pallas_call(
        _peer2_kernel,
        grid=(T // tq,),
        in_specs=[
            pl.BlockSpec((tq, N_PICKS), lambda i: (i, 0), memory_space=pltpu.SMEM),
            row(), row(), tok(),
            pl.BlockSpec((None, CHUNKS, LANES), lambda i: (i // spb, 0, 0)),
            _resident((None,) + tab.shape[1:], lambda i: (l, 0, 0, 0)),
            _resident((N_PICKS, n), lambda i: (0, 0)),
            _resident((N_PICKS, n), lambda i: (0, 0)),
        ],
        out_specs=tok(),
        out_shape=jax.ShapeDtypeStruct((T, CHUNKS, LANES), F32),
        scratch_shapes=[pltpu.VMEM((tq, n), F32)],
        compiler_params=_cparams(48, ("arbitrary",)),
        name="peer_combine",
    )(pair, par, coef, x3, g2, tab, ee, eo)


def kernel(x, c, positions, ada_w, ada_b, norm_mix_g, w_in, q_norm_g, k_norm_g, attn_sinks,
           w_o_attn, conv_w, conv_b, conv_ln_g, conv_ln_b, w_o_conv, w_out, norm_ffn_g,
           peer_w_query, peer_sub_keys, peer_u, peer_v):
    B, S, D = x.shape
    depth = ada_w.shape[0]
    T = B * S
    assert D == CHUNKS * LANES and S % 512 == 0

    cos128, sin128 = _rope_tables(positions)
    mod = _modulation(c, ada_w, ada_b)
    mod4 = mod.reshape(depth, B, 1, 6 * D)
    g2_all = mod[:, :, 5 * D:6 * D].reshape(depth, B, CHUNKS, LANES)

    w_in_bf = w_in.astype(BF16)
    w_o_attn_bf = w_o_attn.astype(BF16)
    w_o_conv_bf = w_o_conv.astype(BF16)
    w_out_bf = w_out.astype(BF16)
    w_query_bf = peer_w_query.astype(BF16)
    keys_bf = peer_sub_keys.astype(BF16).reshape(depth, 2 * N_RET_HEADS, N_KEYS, peer_sub_keys.shape[-1])
    n_exp = peer_u.shape[1]
    u_bf = peer_u.astype(BF16).reshape(depth, n_exp // 2, PAIR_ROWS, LANES)
    v_bf = peer_v.astype(BF16).reshape(depth, n_exp // 2, PAIR_ROWS, LANES)
    ge, go, ee, eo = _peer_constants()
    vec3 = lambda a: a.reshape(depth, 1, a.shape[-1])
    gq128 = jnp.concatenate([q_norm_g, q_norm_g], axis=-1).reshape(depth, 1, LANES)
    gk128 = jnp.concatenate([k_norm_g, k_norm_g], axis=-1).reshape(depth, 1, LANES)
    norm_mix3, norm_ffn3 = vec3(norm_mix_g), vec3(norm_ffn_g)
    conv_b3, ln_g3, ln_b3 = vec3(conv_b), vec3(conv_ln_g), vec3(conv_ln_b)

    xt = x.reshape(T, D)
    for l in range(depth):
        q, kv, glu, gates = _in_projection(xt, mod4, norm_mix3, w_in_bf, l, S)
        attn_o = _attention(q, kv, cos128, sin128, gq128[l], gk128[l], attn_sinks[l], B, S)
        conv_o = _conv_branch(glu, conv_w, conv_b3, ln_g3, ln_b3, w_o_conv_bf, l, B, S)
        x_mid, h2 = _merge(xt, attn_o, conv_o, gates, w_o_attn_bf, w_out_bf, mod4, norm_ffn3, l, S)
        pair, par, gate = _route(h2, w_query_bf, keys_bf, l)
        coef = _peer1(pair, par, gate, h2.reshape(T, CHUNKS, LANES), u_bf, ge, go, l)
        x3 = _peer2(pair, par, coef, x_mid.reshape(T, CHUNKS, LANES), g2_all[l], v_bf, ee, eo, l, S)
        xt = x3.reshape(T, D)
    return xt.reshape(B, S, D)
```

```python
import numpy as np
import jax
import jax.numpy as jnp
from jax import lax
from jax.experimental import pallas as pl
from jax.experimental.pallas import tpu as pltpu

F32 = jnp.float32
BF16 = jnp.bfloat16

HEAD_DIM = 64
N_Q_HEADS = 16
N_KV_HEADS = 4
Q_PER_KV = N_Q_HEADS // N_KV_HEADS
WINDOW = 128
ROPE_THETA = 10000.0
CONV_WIDTH = 31
N_KEYS = 128
N_RET_HEADS = 8
TOPK = 16
N_PICKS = N_RET_HEADS * TOPK
EPS = 1e-6
NEG_INF = -1e30

LANES = 128
PAIR_ROWS = 16
PAIR_WORD_ROWS = PAIR_ROWS // 2
CHUNKS = 8
PICKS_PER_TILE = 32
N_TILES = N_PICKS // PICKS_PER_TILE
HALF_TILE = PICKS_PER_TILE // 2


def _cparams(vmem_mb, sem):
    return pltpu.CompilerParams(dimension_semantics=sem, vmem_limit_bytes=vmem_mb * 1024 * 1024)


def _resident(block_shape, index_map):
    return pl.BlockSpec(block_shape, index_map, pipeline_mode=pl.Buffered(1))


def _split_bf16(v):
    hi = v.astype(BF16)
    lo = (v - hi.astype(F32)).astype(BF16)
    return hi, lo


def _dot(a, b):
    return jnp.dot(a, b, preferred_element_type=F32)


def _dot_nt(a, b):
    return lax.dot_general(a, b, (((1,), (1,)), ((), ())), preferred_element_type=F32)


def _rope_kernel(pos_ref, inv_ref, cs_ref):
    ang = inv_ref[...] * pos_ref[...].astype(F32)
    cs_ref[0:32, :] = jnp.cos(ang)
    cs_ref[32:64, :] = jnp.sin(ang)


def _rope_tables(positions):
    T = positions.size
    tb = min(2048, T)
    inv = ROPE_THETA ** (-jnp.arange(0, HEAD_DIM, 2, dtype=F32) / HEAD_DIM)
    cs = pl.pallas_call(
        _rope_kernel,
        grid=(T // tb,),
        in_specs=[pl.BlockSpec((1, tb), lambda i: (0, i)), pl.BlockSpec((32, 1), lambda i: (0, 0))],
        out_specs=pl.BlockSpec((64, tb), lambda i: (0, i)),
        out_shape=jax.ShapeDtypeStruct((64, T), F32),
        compiler_params=_cparams(32, ("arbitrary",)),
        name="rope_tables",
    )(positions.reshape(1, T), inv.reshape(32, 1))
    cos = cs[0:32].T
    sin = cs[32:64].T
    cos128 = jnp.concatenate([cos, cos, cos, cos], axis=1)
    sin128 = jnp.concatenate([-sin, sin, -sin, sin], axis=1)
    return cos128, sin128


def _mod_kernel(c_ref, w_ref, b_ref, o_ref):
    c = c_ref[...]
    ca = c * jax.nn.sigmoid(c)
    o_ref[...] = jnp.dot(ca, w_ref[...], preferred_element_type=F32,
                         precision=lax.Precision.HIGHEST) + b_ref[...]


def _modulation(c, ada_w, ada_b):
    depth, D, six_d = ada_w.shape
    B = c.shape[0]
    tn = 1024
    return pl.pallas_call(
        _mod_kernel,
        grid=(depth, six_d // tn),
        in_specs=[
            pl.BlockSpec((B, D), lambda l, j: (0, 0)),
            pl.BlockSpec((None, D, tn), lambda l, j: (l, 0, j)),
            pl.BlockSpec((None, 1, tn), lambda l, j: (l, 0, j)),
        ],
        out_specs=pl.BlockSpec((None, B, tn), lambda l, j: (l, 0, j)),
        out_shape=jax.ShapeDtypeStruct((depth, B, six_d), F32),
        compiler_params=_cparams(32, ("arbitrary", "arbitrary")),
        name="adaln_modulation",
    )(c, ada_w, ada_b.reshape(depth, 1, six_d))


def _in_kernel(x_ref, mod_ref, g_ref, w_ref, q_ref, kv_ref, glu_ref, gates_ref):
    D = x_ref.shape[1]
    x = x_ref[...]
    y = x * lax.rsqrt(jnp.mean(x * x, axis=-1, keepdims=True) + EPS) * g_ref[...]
    h = (y * (1.0 + mod_ref[:, D:2 * D]) + mod_ref[:, 0:D]).astype(BF16)
    c0 = 0
    for ref in (q_ref, kv_ref, glu_ref, gates_ref):
        n = ref.shape[1]
        ref[...] = _dot(h, w_ref[:, c0:c0 + n])
        c0 += n


def _in_projection(xt, mod4, norm_g, w_in_bf, l, S):
    T, D = xt.shape
    n_in = w_in_bf.shape[2]
    tm = 256
    q_w = N_Q_HEADS * HEAD_DIM
    kv_w = 2 * N_KV_HEADS * HEAD_DIM
    rest = (n_in - q_w - kv_w) // 2
    spb = S // tm
    widths = (q_w, kv_w, rest, rest)
    return pl.pallas_call(
        _in_kernel,
        grid=(T // tm,),
        in_specs=[
            pl.BlockSpec((tm, D), lambda i: (i, 0)),
            pl.BlockSpec((None, None, 1, mod4.shape[3]), lambda i: (l, i // spb, 0, 0)),
            pl.BlockSpec((None, 1, D), lambda i: (l, 0, 0)),
            _resident((None, D, n_in), lambda i: (l, 0, 0)),
        ],
        out_specs=[pl.BlockSpec((tm, w), lambda i: (i, 0)) for w in widths],
        out_shape=[jax.ShapeDtypeStruct((T, w), F32) for w in widths],
        compiler_params=_cparams(48, ("arbitrary",)),
        name="in_projection",
    )(xt, mod4, norm_g, w_in_bf)


def _norm_rope_chunk(raw, g, cos, sin, lane):
    sq = raw * raw
    low = lane < HEAD_DIM
    ss0 = jnp.sum(jnp.where(low, sq, 0.0), axis=-1, keepdims=True)
    ss1 = jnp.sum(jnp.where(low, 0.0, sq), axis=-1, keepdims=True)
    scale = jnp.where(low, lax.rsqrt(ss0 / HEAD_DIM + EPS), lax.rsqrt(ss1 / HEAD_DIM + EPS))
    xg = raw * scale * g
    first = (lane % HEAD_DIM) < (HEAD_DIM // 2)
    rot = jnp.where(first, pltpu.roll(xg, LANES - HEAD_DIM // 2, 1), pltpu.roll(xg, HEAD_DIM // 2, 1))
    return xg * cos + rot * sin


def _attn_kernel(sink_ref, q_ref, kvc_ref, kvp_ref, cosc_ref, sinc_ref, cosp_ref, sinp_ref,
                 gq_ref, gk_ref, o_ref, kd_scr, vd_scr):
    tq = q_ref.shape[0]
    blk = WINDOW
    j = pl.program_id(1)
    kvw = N_KV_HEADS * HEAD_DIM
    gq = gq_ref[...]
    gk = gk_ref[...]

    for (src, cos_r, sin_r, r0, n) in ((kvp_ref, cosp_ref, sinp_ref, 0, blk),
                                       (kvc_ref, cosc_ref, sinc_ref, blk, tq)):
        lane = lax.broadcasted_iota(jnp.int32, (n, LANES), 1)
        low = lane < HEAD_DIM
        cos = cos_r[...]
        sin = sin_r[...]
        for c in range(kvw // LANES):
            kc = _norm_rope_chunk(src[:, c * LANES:(c + 1) * LANES], gk, cos, sin, lane)
            vc = src[:, kvw + c * LANES:kvw + (c + 1) * LANES]
            kc_sw = pltpu.roll(kc, HEAD_DIM, 1)
            vc_sw = pltpu.roll(vc, HEAD_DIM, 1)
            kd_scr[2 * c, r0:r0 + n, :] = jnp.where(low, kc, kc_sw).astype(BF16)
            kd_scr[2 * c + 1, r0:r0 + n, :] = jnp.where(low, kc_sw, kc).astype(BF16)
            vd_scr[2 * c, r0:r0 + n, :] = jnp.where(low, vc, vc_sw).astype(BF16)
            vd_scr[2 * c + 1, r0:r0 + n, :] = jnp.where(low, vc_sw, vc).astype(BF16)

    lane_q = lax.broadcasted_iota(jnp.int32, (blk, LANES), 1)
    low_q = lane_q < HEAD_DIM
    qi = lax.broadcasted_iota(jnp.int32, (blk, 2 * blk), 0)
    kj = lax.broadcasted_iota(jnp.int32, (blk, 2 * blk), 1)
    diff = qi + blk - kj
    in_window = (diff >= 0) & (diff < WINDOW)
    scale = HEAD_DIM ** -0.5

    for n in range(tq // blk):
        rows = slice(n * blk, (n + 1) * blk)
        cos = cosc_ref[rows, :]
        sin = sinc_ref[rows, :]
        valid = in_window
        if n == 0:
            valid = valid & (kj >= jnp.where(j > 0, 0, blk))
        valid4 = jnp.concatenate([valid] * Q_PER_KV, axis=0)
        for g in range(N_KV_HEADS):
            qm = []
            for c in range(2 * g, 2 * g + 2):
                qc = _norm_rope_chunk(q_ref[rows, c * LANES:(c + 1) * LANES], gq, cos, sin, lane_q) * scale
                qm.append(jnp.where(low_q, qc, 0.0).astype(BF16))
                qm.append(jnp.where(low_q, 0.0, qc).astype(BF16))
            q4 = jnp.concatenate(qm, axis=0)
            kd = kd_scr[g, n * blk:(n + 2) * blk, :]
            vd = vd_scr[g, n * blk:(n + 2) * blk, :]
            s = _dot_nt(q4, kd)
            s = jnp.where(valid4, s, NEG_INF)
            sink = jnp.concatenate(
                [jnp.full((blk, 1), sink_ref[g * Q_PER_KV + r], F32) for r in range(Q_PER_KV)], axis=0)
            m = jnp.maximum(jnp.max(s, axis=-1, keepdims=True), sink)
            p = jnp.exp(s - m)
            p = p / (jnp.sum(p, axis=-1, keepdims=True) + jnp.exp(sink - m))
            o4 = _dot(p.astype(BF16), vd)
            for r2 in range(Q_PER_KV // 2):
                c = (g * Q_PER_KV) // 2 + r2
                oe = o4[(2 * r2) * blk:(2 * r2 + 1) * blk, :]
                oo = o4[(2 * r2 + 1) * blk:(2 * r2 + 2) * blk, :]
                o_ref[rows, c * LANES:(c + 1) * LANES] = jnp.where(low_q, oe, oo).astype(BF16)


def _attention(q, kv, cos128, sin128, gq, gk, sinks, B, S):
    T = q.shape[0]
    tq = 512
    blk = WINDOW
    nq = S // tq
    bpq = tq // blk
    cur = lambda b, j: (b * nq + j, 0)
    prev = lambda b, j: (jnp.maximum((b * nq + j) * bpq - 1, 0), 0)
    kvw = kv.shape[1]
    return pl.pallas_call(
        _attn_kernel,
        grid=(B, nq),
        in_specs=[
            pl.BlockSpec(memory_space=pltpu.SMEM),
            pl.BlockSpec((tq, q.shape[1]), cur),
            pl.BlockSpec((tq, kvw), cur),
            pl.BlockSpec((blk, kvw), prev),
            pl.BlockSpec((tq, LANES), cur),
            pl.BlockSpec((tq, LANES), cur),
            pl.BlockSpec((blk, LANES), prev),
            pl.BlockSpec((blk, LANES), prev),
            pl.BlockSpec((1, LANES), lambda b, j: (0, 0)),
            pl.BlockSpec((1, LANES), lambda b, j: (0, 0)),
        ],
        out_specs=pl.BlockSpec((tq, q.shape[1]), cur),
        scratch_shapes=[pltpu.VMEM((N_KV_HEADS, tq + blk, LANES), BF16),
                        pltpu.VMEM((N_KV_HEADS, tq + blk, LANES), BF16)],
        out_shape=jax.ShapeDtypeStruct((T, q.shape[1]), BF16),
        compiler_params=_cparams(48, ("arbitrary", "arbitrary")),
        name="swa_attention",
    )(sinks, q, kv, kv, cos128, sin128, cos128, sin128, gq, gk)


HALO = 32


def _conv_kernel(glu_ref, cw_ref, cb_ref, lg_ref, lb_ref, wo_ref, o_ref, ubuf):
    tc = glu_ref.shape[0]
    C = o_ref.shape[1]
    j = pl.program_id(1)

    @pl.when(j == 0)
    def _():
        ubuf[0:HALO, :] = jnp.zeros((HALO, C), F32)

    @pl.when(j > 0)
    def _():
        ubuf[0:HALO, :] = ubuf[tc:tc + HALO, :]

    ubuf[HALO:HALO + tc, :] = glu_ref[:, 0:C] * jax.nn.sigmoid(glu_ref[:, C:2 * C])
    off = HALO - (CONV_WIDTH - 1)
    acc = cb_ref[...] + cw_ref[0:1, :] * ubuf[off:off + tc, :]
    for k in range(1, CONV_WIDTH):
        acc = acc + cw_ref[k:k + 1, :] * ubuf[off + k:off + k + tc, :]
    mu = jnp.mean(acc, axis=-1, keepdims=True)
    d = acc - mu
    var = jnp.mean(d * d, axis=-1, keepdims=True)
    yn = d * lax.rsqrt(var + EPS) * lg_ref[...] + lb_ref[...]
    act = yn * jax.nn.sigmoid(yn)
    o_ref[...] = _dot(act.astype(BF16), wo_ref[...])


def _conv_branch(glu, conv_w, conv_b, ln_g, ln_b, w_o_conv_bf, l, B, S):
    T = glu.shape[0]
    C = glu.shape[1] // 2
    tc = 256
    nc = S // tc
    vec = lambda: pl.BlockSpec((None, 1, C), lambda b, j: (l, 0, 0))
    return pl.pallas_call(
        _conv_kernel,
        grid=(B, nc),
        in_specs=[
            pl.BlockSpec((tc, 2 * C), lambda b, j: (b * nc + j, 0)),
            pl.BlockSpec((None, CONV_WIDTH, C), lambda b, j: (l, 0, 0)),
            vec(), vec(), vec(),
            _resident((None, C, C), lambda b, j: (l, 0, 0)),
        ],
        out_specs=pl.BlockSpec((tc, C), lambda b, j: (b * nc + j, 0)),
        out_shape=jax.ShapeDtypeStruct((T, C), F32),
        scratch_shapes=[pltpu.VMEM((tc + HALO, C), F32)],
        compiler_params=_cparams(48, ("arbitrary", "arbitrary")),
        name="conv_branch",
    )(glu, conv_w, conv_b, ln_g, ln_b, w_o_conv_bf)


def _merge_kernel(x_ref, ao_ref, co_ref, gates_ref, woa_ref, wout_ref, mod_ref, gf_ref,
                  xmid_ref, h2_ref):
    D = x_ref.shape[1]
    attn = _dot(ao_ref[...], woa_ref[...])
    merged = (jax.nn.sigmoid(gates_ref[:, 0:D]) * attn
              + jax.nn.sigmoid(gates_ref[:, D:2 * D]) * co_ref[...])
    out = _dot(merged.astype(BF16), wout_ref[...])
    xm = x_ref[...] + mod_ref[:, 2 * D:3 * D] * out
    xmid_ref[...] = xm
    y = xm * lax.rsqrt(jnp.mean(xm * xm, axis=-1, keepdims=True) + EPS) * gf_ref[...]
    h2_ref[...] = y * (1.0 + mod_ref[:, 4 * D:5 * D]) + mod_ref[:, 3 * D:4 * D]


def _merge(xt, attn_o, conv_o, gates, w_o_attn_bf, w_out_bf, mod4, norm_ffn_g, l, S):
    T, D = xt.shape
    tm = 256
    spb = S // tm
    row = lambda w: pl.BlockSpec((tm, w), lambda i: (i, 0))
    return pl.pallas_call(
        _merge_kernel,
        grid=(T // tm,),
        in_specs=[
            row(D), row(D), row(D), row(2 * D),
            _resident((None, D, D), lambda i: (l, 0, 0)),
            _resident((None, D, D), lambda i: (l, 0, 0)),
            pl.BlockSpec((None, None, 1, mod4.shape[3]), lambda i: (l, i // spb, 0, 0)),
            pl.BlockSpec((None, 1, D), lambda i: (l, 0, 0)),
        ],
        out_specs=[row(D), row(D)],
        out_shape=[jax.ShapeDtypeStruct((T, D), F32), jax.ShapeDtypeStruct((T, D), F32)],
        compiler_params=_cparams(48, ("arbitrary",)),
        name="merge_out_projection",
    )(xt, attn_o, conv_o, gates, w_o_attn_bf, w_out_bf, mod4, norm_ffn_g)


def _topk_rows(s, payload, k):
    R, n = s.shape
    rid = lax.broadcasted_iota(jnp.int32, (R, n), 0).astype(F32)
    vals, pays = [], []
    for _ in range(k):
        m = jnp.max(s, axis=0, keepdims=True)
        pos = jnp.min(jnp.where(s == m, rid, float(R)), axis=0, keepdims=True)
        sel = rid == pos
        if payload is None:
            pays.append(pos)
        else:
            pays.append(jnp.max(jnp.where(sel, payload, -1.0), axis=0, keepdims=True))
        vals.append(m)
        s = jnp.where(sel, -jnp.inf, s)
    return jnp.concatenate(vals, axis=0), jnp.concatenate(pays, axis=0)


_CANDS = [(a, b) for a in range(TOPK) for b in range(TOPK) if (a + 1) * (b + 1) <= TOPK]
_SUBLANES = 8


def _cand_rows(v0, v1):
    n_pad = -len(_CANDS) % _SUBLANES
    ra = jnp.concatenate([v0[a:a + 1, :] for a, _ in _CANDS] + [v0[0:1, :]] * n_pad, axis=0)
    rb = jnp.concatenate([v1[b:b + 1, :] for _, b in _CANDS] + [v1[0:1, :]] * n_pad, axis=0)
    row = lax.broadcasted_iota(jnp.int32, ra.shape, 0)
    return ra, rb, row < len(_CANDS)


def _route_kernel(h_ref, wq_ref, keys_ref, off_ref, par_ref, gate_ref):
    q = _dot(h_ref[...].astype(BF16), wq_ref[...])
    ids_all, gates_all = [], []
    for h in range(N_RET_HEADS):
        half = []
        for p in range(2):
            hp = 2 * h + p
            qs = q[:, hp * N_KEYS:(hp + 1) * N_KEYS].astype(BF16)
            st = _dot_nt(keys_ref[hp], qs)
            half.append(_topk_rows(st, None, TOPK))
        (v0, i0), (v1, i1) = half
        sa, sb, real = _cand_rows(v0, v1)
        ia, ib, _ = _cand_rows(i0, i1)
        cand_s = jnp.where(real, sa + sb, -jnp.inf)
        cand_i = ia * float(N_KEYS) + ib
        top_s, top_i = _topk_rows(cand_s, cand_i, TOPK)
        e = jnp.exp(top_s - top_s[0:1, :])
        gates_all.append(e / jnp.sum(e, axis=0, keepdims=True))
        ids_all.append(top_i)
    ids = jnp.concatenate(ids_all, axis=0).T.astype(jnp.int32)
    gate_ref[...] = jnp.concatenate(gates_all, axis=0).T
    off_ref[...] = lax.shift_right_logical(ids, 1) * PAIR_WORD_ROWS
    par_ref[...] = ids & 1


def _route(h2, w_query_bf, keys_bf, l):
    T, D = h2.shape
    tr = 128
    nq = w_query_bf.shape[2]
    out = lambda dt: jax.ShapeDtypeStruct((T, N_PICKS), dt)
    return pl.pallas_call(
        _route_kernel,
        grid=(T // tr,),
        in_specs=[
            pl.BlockSpec((tr, D), lambda i: (i, 0)),
            _resident((None, D, nq), lambda i: (l, 0, 0)),
            _resident((None, 2 * N_RET_HEADS, N_KEYS, N_KEYS), lambda i: (l, 0, 0, 0)),
        ],
        out_specs=[pl.BlockSpec((tr, N_PICKS), lambda i: (i, 0))] * 3,
        out_shape=[out(jnp.int32), out(jnp.int32), out(F32)],
        compiler_params=_cparams(32, ("arbitrary",)),
        name="peer_route",
    )(h2, w_query_bf, keys_bf)


TOKENS_PER_ITER = 8


def _pack_table(tab):
    depth, n_exp, d = tab.shape
    rows = tab.astype(BF16).reshape(depth, n_exp * d // (2 * LANES), 2, LANES)
    return lax.bitcast_convert_type(jnp.swapaxes(rows, -1, -2), jnp.uint32)


def _pair_vregs(tab_ref, off_ref, t, first, count):
    words = jnp.concatenate(
        [tab_ref[pl.ds(pl.multiple_of(off_ref[t, first + i], PAIR_WORD_ROWS), PAIR_WORD_ROWS), :]
         for i in range(count)], axis=0)
    return pltpu.bitcast(words, BF16)


def _tile_weights(tab_ref, off_ref, t, tile):
    base = tile * PICKS_PER_TILE
    wa = _pair_vregs(tab_ref, off_ref, t, base, HALF_TILE)
    wb = _pair_vregs(tab_ref, off_ref, t, base + HALF_TILE, HALF_TILE)
    return jnp.concatenate([wa, wb], axis=1)


def _token_loop(n_tokens, token):
    def body(i, carry):
        for r in range(TOKENS_PER_ITER):
            token(i * TOKENS_PER_ITER + r)
        return carry

    lax.fori_loop(0, n_tokens // TOKENS_PER_ITER, body, 0)


def _peer1_kernel(pair_ref, par_ref, gate_ref, h_ref, tab_ref, ge_ref, go_ref, coef_ref, u_scr):
    tq = h_ref.shape[0]
    tile_w = 2 * HALF_TILE * PAIR_ROWS
    half_w = HALF_TILE * PAIR_ROWS
    sub = lax.broadcasted_iota(jnp.int32, (CHUNKS, half_w), 0)
    lane = lax.broadcasted_iota(jnp.int32, (CHUNKS, half_w), 1)
    diag = (lane % CHUNKS) == sub
    zero = jnp.zeros((CHUNKS, LANES), BF16)

    def token(t):
        xhi, xlo = _split_bf16(h_ref[t])
        lhs = jnp.concatenate([
            jnp.concatenate([xhi, zero], axis=1), jnp.concatenate([xlo, zero], axis=1),
            jnp.concatenate([zero, xhi], axis=1), jnp.concatenate([zero, xlo], axis=1)], axis=0)
        for tile in range(N_TILES):
            s = _dot_nt(lhs, _tile_weights(tab_ref, pair_ref, t, tile))
            sa = s[0:CHUNKS] + s[CHUNKS:2 * CHUNKS]
            sb = s[2 * CHUNKS:3 * CHUNKS] + s[3 * CHUNKS:4 * CHUNKS]
            ua = jnp.sum(jnp.where(diag, sa, 0.0), axis=0, keepdims=True)
            ub = jnp.sum(jnp.where(diag, sb, 0.0), axis=0, keepdims=True)
            u_scr[pl.ds(t, 1), tile * tile_w:tile * tile_w + half_w] = ua
            u_scr[pl.ds(t, 1), tile * tile_w + half_w:(tile + 1) * tile_w] = ub

    _token_loop(tq, token)
    uhi, ulo = _split_bf16(u_scr[...])
    a_even = _dot(uhi, ge_ref[...]) + _dot(ulo, ge_ref[...])
    a_odd = _dot(uhi, go_ref[...]) + _dot(ulo, go_ref[...])
    a = jnp.where(par_ref[...] == 1, a_odd, a_even)
    gelu = 0.5 * a * (1.0 + lax.erf(a * (2.0 ** -0.5)))
    coef_ref[...] = gate_ref[...] * gelu


def _peer2_kernel(pair_ref, par_ref, coef_ref, x_ref, g2_ref, tab_ref, ee_ref, eo_ref, o_ref, c_scr):
    tq = x_ref.shape[0]
    half_w = HALF_TILE * PAIR_ROWS
    tile_w = 2 * half_w
    c = coef_ref[...]
    odd = par_ref[...] == 1
    cehi, celo = _split_bf16(jnp.where(odd, 0.0, c))
    cohi, colo = _split_bf16(jnp.where(odd, c, 0.0))
    c_scr[...] = (_dot(cehi, ee_ref[...]) + _dot(celo, ee_ref[...])
                  + _dot(cohi, eo_ref[...]) + _dot(colo, eo_ref[...]))
    sub = lax.broadcasted_iota(jnp.int32, (PAIR_ROWS, half_w), 0)
    lane = lax.broadcasted_iota(jnp.int32, (PAIR_ROWS, half_w), 1)
    diag = (lane % PAIR_ROWS) == sub
    g2 = g2_ref[...]

    def token(t):
        acc = jnp.zeros((4 * PAIR_ROWS, 2 * LANES), F32)
        for tile in range(N_TILES):
            parts = []
            for half in range(2):
                r = c_scr[pl.ds(t, 1), tile * tile_w + half * half_w:tile * tile_w + (half + 1) * half_w]
                lmat = jnp.where(diag, jnp.broadcast_to(r, (PAIR_ROWS, half_w)), 0.0)
                parts.extend(_split_bf16(lmat))
            lhs = jnp.concatenate(parts, axis=0)
            acc = acc + _dot(lhs, _tile_weights(tab_ref, pair_ref, t, tile))
        y16 = (acc[0:PAIR_ROWS, 0:LANES] + acc[PAIR_ROWS:2 * PAIR_ROWS, 0:LANES]
               + acc[2 * PAIR_ROWS:3 * PAIR_ROWS, LANES:] + acc[3 * PAIR_ROWS:, LANES:])
        y = y16[0:CHUNKS] + y16[CHUNKS:]
        o_ref[t] = x_ref[t] + g2 * y

    _token_loop(tq, token)


def _peer_constants():
    n = N_PICKS * PAIR_ROWS
    col = np.arange(n)
    pick = col // PAIR_ROWS
    upper = (col % PAIR_ROWS) >= CHUNKS
    sel = pick[:, None] == np.arange(N_PICKS)[None, :]
    ge = (sel & ~upper[:, None]).astype(np.float32)
    go = (sel & upper[:, None]).astype(np.float32)
    as_bf = lambda a: jnp.asarray(a, dtype=BF16)
    return as_bf(ge), as_bf(go), as_bf(ge.T), as_bf(go.T)


def _peer_tokens_per_step():
    return 256


def _peer1(pair, par, gate, h3, tab, ge, go, l):
    T = pair.shape[0]
    tq = _peer_tokens_per_step()
    n = N_PICKS * PAIR_ROWS
    row = lambda: pl.BlockSpec((tq, N_PICKS), lambda i: (i, 0))
    return pl.pallas_call(
        _peer1_kernel,
        grid=(T // tq,),
        in_specs=[
            pl.BlockSpec((tq, N_PICKS), lambda i: (i, 0), memory_space=pltpu.SMEM),
            row(), row(),
            pl.BlockSpec((tq, CHUNKS, LANES), lambda i: (i, 0, 0)),
            _resident((None,) + tab.shape[1:], lambda i: (l, 0, 0)),
            _resident((n, N_PICKS), lambda i: (0, 0)),
            _resident((n, N_PICKS), lambda i: (0, 0)),
        ],
        out_specs=row(),
        out_shape=jax.ShapeDtypeStruct((T, N_PICKS), F32),
        scratch_shapes=[pltpu.VMEM((tq, n), F32)],
        compiler_params=_cparams(48, ("arbitrary",)),
        name="peer_scores",
    )(pair, par, gate, h3, tab, ge, go)


def _peer2(pair, par, coef, x3, g2, tab, ee, eo, l, S):
    T = pair.shape[0]
    tq = _peer_tokens_per_step()
    n = N_PICKS * PAIR_ROWS
    spb = S // tq
    row = lambda: pl.BlockSpec((tq, N_PICKS), lambda i: (i, 0))
    tok = lambda: pl.BlockSpec((tq, CHUNKS, LANES), lambda i: (i, 0, 0))
    return pl.pallas_call(
        _peer2_kernel,
        grid=(T // tq,),
        in_specs=[
            pl.BlockSpec((tq, N_PICKS), lambda i: (i, 0), memory_space=pltpu.SMEM),
            row(), row(), tok(),
            pl.BlockSpec((None, CHUNKS, LANES), lambda i: (i // spb, 0, 0)),
            _resident((None,) + tab.shape[1:], lambda i: (l, 0, 0)),
            _resident((N_PICKS, n), lambda i: (0, 0)),
            _resident((N_PICKS, n), lambda i: (0, 0)),
        ],
        out_specs=tok(),
        out_shape=jax.ShapeDtypeStruct((T, CHUNKS, LANES), F32),
        scratch_shapes=[pltpu.VMEM((tq, n), F32)],
        compiler_params=_cparams(48, ("arbitrary",)),
        name="peer_combine",
    )(pair, par, coef, x3, g2, tab, ee, eo)


def kernel(x, c, positions, ada_w, ada_b, norm_mix_g, w_in, q_norm_g, k_norm_g, attn_sinks,
           w_o_attn, conv_w, conv_b, conv_ln_g, conv_ln_b, w_o_conv, w_out, norm_ffn_g,
           peer_w_query, peer_sub_keys, peer_u, peer_v):
    B, S, D = x.shape
    depth = ada_w.shape[0]
    T = B * S
    assert D == CHUNKS * LANES and S % 512 == 0

    cos128, sin128 = _rope_tables(positions)
    mod = _modulation(c, ada_w, ada_b)
    mod4 = mod.reshape(depth, B, 1, 6 * D)
    g2_all = mod[:, :, 5 * D:6 * D].reshape(depth, B, CHUNKS, LANES)

    w_in_bf = w_in.astype(BF16)
    w_o_attn_bf = w_o_attn.astype(BF16)
    w_o_conv_bf = w_o_conv.astype(BF16)
    w_out_bf = w_out.astype(BF16)
    w_query_bf = peer_w_query.astype(BF16)
    keys_bf = peer_sub_keys.astype(BF16).reshape(depth, 2 * N_RET_HEADS, N_KEYS, peer_sub_keys.shape[-1])
    u_bf = _pack_table(peer_u)
    v_bf = _pack_table(peer_v)
    ge, go, ee, eo = _peer_constants()
    vec3 = lambda a: a.reshape(depth, 1, a.shape[-1])
    gq128 = jnp.concatenate([q_norm_g, q_norm_g], axis=-1).reshape(depth, 1, LANES)
    gk128 = jnp.concatenate([k_norm_g, k_norm_g], axis=-1).reshape(depth, 1, LANES)
    norm_mix3, norm_ffn3 = vec3(norm_mix_g), vec3(norm_ffn_g)
    conv_b3, ln_g3, ln_b3 = vec3(conv_b), vec3(conv_ln_g), vec3(conv_ln_b)

    xt = x.reshape(T, D)
    for l in range(depth):
        q, kv, glu, gates = _in_projection(xt, mod4, norm_mix3, w_in_bf, l, S)
        attn_o = _attention(q, kv, cos128, sin128, gq128[l], gk128[l], attn_sinks[l], B, S)
        conv_o = _conv_branch(glu, conv_w, conv_b3, ln_g3, ln_b3, w_o_conv_bf, l, B, S)
        x_mid, h2 = _merge(xt, attn_o, conv_o, gates, w_o_attn_bf, w_out_bf, mod4, norm_ffn3, l, S)
        pair, par, gate = _route(h2, w_query_bf, keys_bf, l)
        coef = _peer1(pair, par, gate, h2.reshape(T, CHUNKS, LANES), u_bf, ge, go, l)
        x3 = _peer2(pair, par, coef, x_mid.reshape(T, CHUNKS, LANES), g2_all[l], v_bf, ee, eo, l, S)
        xt = x3.reshape(T, D)
    return xt.reshape(B, S, D)
```

```python
import numpy as np
import jax
import jax.numpy as jnp
from jax import lax
from jax.experimental import pallas as pl
from jax.experimental.pallas import tpu as pltpu

F32 = jnp.float32
BF16 = jnp.bfloat16

HEAD_DIM = 64
N_Q_HEADS = 16
N_KV_HEADS = 4
Q_PER_KV = N_Q_HEADS // N_KV_HEADS
WINDOW = 128
ROPE_THETA = 10000.0
CONV_WIDTH = 31
N_KEYS = 128
N_RET_HEADS = 8
TOPK = 16
N_PICKS = N_RET_HEADS * TOPK
EPS = 1e-6
NEG_INF = -1e30

LANES = 128
SUBLANES = 8
PAIR_ROWS = 16
PAIR_WORD_ROWS = PAIR_ROWS // 2
CHUNKS = 8
PICKS_PER_TILE = 32
N_TILES = N_PICKS // PICKS_PER_TILE
HALF_TILE = PICKS_PER_TILE // 2


def _cparams(vmem_mb, sem):
    return pltpu.CompilerParams(dimension_semantics=sem, vmem_limit_bytes=vmem_mb * 1024 * 1024)


def _resident(block_shape, index_map):
    return pl.BlockSpec(block_shape, index_map, pipeline_mode=pl.Buffered(1))


def _split_bf16(v):
    hi = v.astype(BF16)
    lo = (v - hi.astype(F32)).astype(BF16)
    return hi, lo


def _dot(a, b):
    return jnp.dot(a, b, preferred_element_type=F32)


def _dot_nt(a, b):
    return lax.dot_general(a, b, (((1,), (1,)), ((), ())), preferred_element_type=F32)


def _rope_kernel(pos_ref, inv_ref, cs_ref):
    ang = inv_ref[...] * pos_ref[...].astype(F32)
    cs_ref[0:32, :] = jnp.cos(ang)
    cs_ref[32:64, :] = jnp.sin(ang)


def _rope_tables(positions):
    T = positions.size
    tb = min(2048, T)
    inv = ROPE_THETA ** (-jnp.arange(0, HEAD_DIM, 2, dtype=F32) / HEAD_DIM)
    cs = pl.pallas_call(
        _rope_kernel,
        grid=(T // tb,),
        in_specs=[pl.BlockSpec((1, tb), lambda i: (0, i)), pl.BlockSpec((32, 1), lambda i: (0, 0))],
        out_specs=pl.BlockSpec((64, tb), lambda i: (0, i)),
        out_shape=jax.ShapeDtypeStruct((64, T), F32),
        compiler_params=_cparams(32, ("arbitrary",)),
        name="rope_tables",
    )(positions.reshape(1, T), inv.reshape(32, 1))
    cos = cs[0:32].T
    sin = cs[32:64].T
    cos128 = jnp.concatenate([cos, cos, cos, cos], axis=1)
    sin128 = jnp.concatenate([-sin, sin, -sin, sin], axis=1)
    return cos128, sin128


def _mod_kernel(c_ref, w_ref, b_ref, o_ref):
    c = c_ref[...]
    ca = c * jax.nn.sigmoid(c)
    o_ref[...] = jnp.dot(ca, w_ref[...], preferred_element_type=F32,
                         precision=lax.Precision.HIGHEST) + b_ref[...]


def _modulation(c, ada_w, ada_b):
    depth, D, six_d = ada_w.shape
    B = c.shape[0]
    tn = 1024
    return pl.pallas_call(
        _mod_kernel,
        grid=(depth, six_d // tn),
        in_specs=[
            pl.BlockSpec((B, D), lambda l, j: (0, 0)),
            pl.BlockSpec((None, D, tn), lambda l, j: (l, 0, j)),
            pl.BlockSpec((None, 1, tn), lambda l, j: (l, 0, j)),
        ],
        out_specs=pl.BlockSpec((None, B, tn), lambda l, j: (l, 0, j)),
        out_shape=jax.ShapeDtypeStruct((depth, B, six_d), F32),
        compiler_params=_cparams(32, ("arbitrary", "arbitrary")),
        name="adaln_modulation",
    )(c, ada_w, ada_b.reshape(depth, 1, six_d))


def _in_kernel(x_ref, mod_ref, g_ref, w_ref, q_ref, kv_ref, glu_ref, gates_ref):
    D = x_ref.shape[1]
    x = x_ref[...]
    y = x * lax.rsqrt(jnp.mean(x * x, axis=-1, keepdims=True) + EPS) * g_ref[...]
    h = (y * (1.0 + mod_ref[:, D:2 * D]) + mod_ref[:, 0:D]).astype(BF16)
    c0 = 0
    for ref in (q_ref, kv_ref, glu_ref, gates_ref):
        n = ref.shape[1]
        ref[...] = _dot(h, w_ref[:, c0:c0 + n])
        c0 += n


def _in_projection(xt, mod4, norm_g, w_in_bf, l, S):
    T, D = xt.shape
    n_in = w_in_bf.shape[2]
    tm = 256
    q_w = N_Q_HEADS * HEAD_DIM
    kv_w = 2 * N_KV_HEADS * HEAD_DIM
    rest = (n_in - q_w - kv_w) // 2
    spb = S // tm
    widths = (q_w, kv_w, rest, rest)
    return pl.pallas_call(
        _in_kernel,
        grid=(T // tm,),
        in_specs=[
            pl.BlockSpec((tm, D), lambda i: (i, 0)),
            pl.BlockSpec((None, None, 1, mod4.shape[3]), lambda i: (l, i // spb, 0, 0)),
            pl.BlockSpec((None, 1, D), lambda i: (l, 0, 0)),
            _resident((None, D, n_in), lambda i: (l, 0, 0)),
        ],
        out_specs=[pl.BlockSpec((tm, w), lambda i: (i, 0)) for w in widths],
        out_shape=[jax.ShapeDtypeStruct((T, w), F32) for w in widths],
        compiler_params=_cparams(48, ("arbitrary",)),
        name="in_projection",
    )(xt, mod4, norm_g, w_in_bf)


def _norm_rope_chunk(raw, g, cos, sin, lane):
    sq = raw * raw
    low = lane < HEAD_DIM
    ss0 = jnp.sum(jnp.where(low, sq, 0.0), axis=-1, keepdims=True)
    ss1 = jnp.sum(jnp.where(low, 0.0, sq), axis=-1, keepdims=True)
    scale = jnp.where(low, lax.rsqrt(ss0 / HEAD_DIM + EPS), lax.rsqrt(ss1 / HEAD_DIM + EPS))
    xg = raw * scale * g
    first = (lane % HEAD_DIM) < (HEAD_DIM // 2)
    rot = jnp.where(first, pltpu.roll(xg, LANES - HEAD_DIM // 2, 1), pltpu.roll(xg, HEAD_DIM // 2, 1))
    return xg * cos + rot * sin


def _attn_kernel(sink_ref, q_ref, kvc_ref, kvp_ref, cosc_ref, sinc_ref, cosp_ref, sinp_ref,
                 gq_ref, gk_ref, o_ref, kd_scr, vd_scr, q_scr, s_scr, p_scr):
    tq = q_ref.shape[0]
    blk = WINDOW
    j = pl.program_id(1)
    kvw = N_KV_HEADS * HEAD_DIM
    gq = gq_ref[...]
    gk = gk_ref[...]

    for (src, cos_r, sin_r, r0, n) in ((kvp_ref, cosp_ref, sinp_ref, 0, blk),
                                       (kvc_ref, cosc_ref, sinc_ref, blk, tq)):
        lane = lax.broadcasted_iota(jnp.int32, (n, LANES), 1)
        low = lane < HEAD_DIM
        cos = cos_r[...]
        sin = sin_r[...]
        for c in range(kvw // LANES):
            kc = _norm_rope_chunk(src[:, c * LANES:(c + 1) * LANES], gk, cos, sin, lane)
            vc = src[:, kvw + c * LANES:kvw + (c + 1) * LANES]
            kc_sw = pltpu.roll(kc, HEAD_DIM, 1)
            vc_sw = pltpu.roll(vc, HEAD_DIM, 1)
            kd_scr[2 * c, r0:r0 + n, :] = jnp.where(low, kc, kc_sw).astype(BF16)
            kd_scr[2 * c + 1, r0:r0 + n, :] = jnp.where(low, kc_sw, kc).astype(BF16)
            vd_scr[2 * c, r0:r0 + n, :] = jnp.where(low, vc, vc_sw).astype(BF16)
            vd_scr[2 * c + 1, r0:r0 + n, :] = jnp.where(low, vc_sw, vc).astype(BF16)

    scale = HEAD_DIM ** -0.5
    lane_t = lax.broadcasted_iota(jnp.int32, (tq, LANES), 1)
    low_t = lane_t < HEAD_DIM
    for c in range(N_Q_HEADS // 2):
        qc = _norm_rope_chunk(q_ref[:, c * LANES:(c + 1) * LANES], gq, cosc_ref[...], sinc_ref[...],
                              lane_t) * scale
        q_scr[2 * c] = jnp.where(low_t, qc, 0.0).astype(BF16)
        q_scr[2 * c + 1] = jnp.where(low_t, 0.0, qc).astype(BF16)

    low_q = lax.broadcasted_iota(jnp.int32, (blk, LANES), 1) < HEAD_DIM
    qi = lax.broadcasted_iota(jnp.int32, (blk, 2 * blk), 0)
    kj = lax.broadcasted_iota(jnp.int32, (blk, 2 * blk), 1)
    diff = qi + blk - kj
    in_window = (diff >= 0) & (diff < WINDOW)

    for n in range(tq // blk):
        rows = slice(n * blk, (n + 1) * blk)
        keys = slice(n * blk, (n + 2) * blk)
        valid = in_window
        if n == 0:
            valid = valid & (kj >= jnp.where(j > 0, 0, blk))
        for g in range(N_KV_HEADS):
            q4 = jnp.concatenate([q_scr[g * Q_PER_KV + r, rows, :] for r in range(Q_PER_KV)], axis=0)
            s_scr[g] = _dot_nt(q4, kd_scr[g, keys, :])
        for g in range(N_KV_HEADS):
            for r in range(Q_PER_KV):
                hrows = slice(r * blk, (r + 1) * blk)
                s = jnp.where(valid, s_scr[g, hrows, :], NEG_INF)
                sink = sink_ref[g * Q_PER_KV + r]
                m = jnp.maximum(jnp.max(s, axis=-1, keepdims=True), sink)
                p = jnp.exp(s - m)
                p = p / (jnp.sum(p, axis=-1, keepdims=True) + jnp.exp(sink - m))
                p_scr[g, hrows, :] = p.astype(BF16)
        for g in range(N_KV_HEADS):
            o4 = _dot(p_scr[g], vd_scr[g, keys, :])
            for r2 in range(Q_PER_KV // 2):
                c = (g * Q_PER_KV) // 2 + r2
                oe = o4[(2 * r2) * blk:(2 * r2 + 1) * blk, :]
                oo = o4[(2 * r2 + 1) * blk:(2 * r2 + 2) * blk, :]
                o_ref[rows, c * LANES:(c + 1) * LANES] = jnp.where(low_q, oe, oo).astype(BF16)


def _attention(q, kv, cos128, sin128, gq, gk, sinks, B, S):
    T = q.shape[0]
    tq = 512
    blk = WINDOW
    nq = S // tq
    bpq = tq // blk
    cur = lambda b, j: (b * nq + j, 0)
    prev = lambda b, j: (jnp.maximum((b * nq + j) * bpq - 1, 0), 0)
    kvw = kv.shape[1]
    return pl.pallas_call(
        _attn_kernel,
        grid=(B, nq),
        in_specs=[
            pl.BlockSpec(memory_space=pltpu.SMEM),
            pl.BlockSpec((tq, q.shape[1]), cur),
            pl.BlockSpec((tq, kvw), cur),
            pl.BlockSpec((blk, kvw), prev),
            pl.BlockSpec((tq, LANES), cur),
            pl.BlockSpec((tq, LANES), cur),
            pl.BlockSpec((blk, LANES), prev),
            pl.BlockSpec((blk, LANES), prev),
            pl.BlockSpec((1, LANES), lambda b, j: (0, 0)),
            pl.BlockSpec((1, LANES), lambda b, j: (0, 0)),
        ],
        out_specs=pl.BlockSpec((tq, q.shape[1]), cur),
        scratch_shapes=[pltpu.VMEM((N_KV_HEADS, tq + blk, LANES), BF16),
                        pltpu.VMEM((N_KV_HEADS, tq + blk, LANES), BF16),
                        pltpu.VMEM((N_Q_HEADS, tq, LANES), BF16),
                        pltpu.VMEM((N_KV_HEADS, Q_PER_KV * blk, 2 * blk), F32),
                        pltpu.VMEM((N_KV_HEADS, Q_PER_KV * blk, 2 * blk), BF16)],
        out_shape=jax.ShapeDtypeStruct((T, q.shape[1]), BF16),
        compiler_params=_cparams(48, ("arbitrary", "arbitrary")),
        name="swa_attention",
    )(sinks, q, kv, kv, cos128, sin128, cos128, sin128, gq, gk)


HALO = 32


def _conv_kernel(glu_ref, cw_ref, cb_ref, lg_ref, lb_ref, wo_ref, o_ref, ubuf):
    tc = glu_ref.shape[0]
    C = o_ref.shape[1]
    j = pl.program_id(1)

    @pl.when(j == 0)
    def _():
        ubuf[0:HALO, :] = jnp.zeros((HALO, C), F32)
        ubuf[HALO + tc:, :] = jnp.zeros((SUBLANES, C), F32)

    @pl.when(j > 0)
    def _():
        ubuf[0:HALO, :] = ubuf[tc:tc + HALO, :]

    ubuf[HALO:HALO + tc, :] = glu_ref[:, 0:C] * jax.nn.sigmoid(glu_ref[:, C:2 * C])
    off = HALO - (CONV_WIDTH - 1)
    acc = cb_ref[...]
    for rem in range(SUBLANES):
        part = None
        for k in range(CONV_WIDTH):
            if (off + k) % SUBLANES == rem:
                base = off + k - rem
                term = cw_ref[k:k + 1, :] * ubuf[base:base + tc + SUBLANES, :]
                part = term if part is None else part + term
        acc = acc + part[rem:rem + tc, :]
    mu = jnp.mean(acc, axis=-1, keepdims=True)
    d = acc - mu
    var = jnp.mean(d * d, axis=-1, keepdims=True)
    yn = d * lax.rsqrt(var + EPS) * lg_ref[...] + lb_ref[...]
    act = yn * jax.nn.sigmoid(yn)
    o_ref[...] = _dot(act.astype(BF16), wo_ref[...])


def _conv_branch(glu, conv_w, conv_b, ln_g, ln_b, w_o_conv_bf, l, B, S):
    T = glu.shape[0]
    C = glu.shape[1] // 2
    tc = 256
    nc = S // tc
    vec = lambda: pl.BlockSpec((None, 1, C), lambda b, j: (l, 0, 0))
    return pl.pallas_call(
        _conv_kernel,
        grid=(B, nc),
        in_specs=[
            pl.BlockSpec((tc, 2 * C), lambda b, j: (b * nc + j, 0)),
            pl.BlockSpec((None, CONV_WIDTH, C), lambda b, j: (l, 0, 0)),
            vec(), vec(), vec(),
            _resident((None, C, C), lambda b, j: (l, 0, 0)),
        ],
        out_specs=pl.BlockSpec((tc, C), lambda b, j: (b * nc + j, 0)),
        out_shape=jax.ShapeDtypeStruct((T, C), F32),
        scratch_shapes=[pltpu.VMEM((tc + HALO + SUBLANES, C), F32)],
        compiler_params=_cparams(48, ("arbitrary", "arbitrary")),
        name="conv_branch",
    )(glu, conv_w, conv_b, ln_g, ln_b, w_o_conv_bf)


def _merge_kernel(x_ref, ao_ref, co_ref, gates_ref, woa_ref, wout_ref, mod_ref, gf_ref,
                  xmid_ref, h2_ref):
    D = x_ref.shape[1]
    attn = _dot(ao_ref[...], woa_ref[...])
    merged = (jax.nn.sigmoid(gates_ref[:, 0:D]) * attn
              + jax.nn.sigmoid(gates_ref[:, D:2 * D]) * co_ref[...])
    out = _dot(merged.astype(BF16), wout_ref[...])
    xm = x_ref[...] + mod_ref[:, 2 * D:3 * D] * out
    xmid_ref[...] = xm
    y = xm * lax.rsqrt(jnp.mean(xm * xm, axis=-1, keepdims=True) + EPS) * gf_ref[...]
    h2_ref[...] = y * (1.0 + mod_ref[:, 4 * D:5 * D]) + mod_ref[:, 3 * D:4 * D]


def _merge(xt, attn_o, conv_o, gates, w_o_attn_bf, w_out_bf, mod4, norm_ffn_g, l, S):
    T, D = xt.shape
    tm = 256
    spb = S // tm
    row = lambda w: pl.BlockSpec((tm, w), lambda i: (i, 0))
    return pl.pallas_call(
        _merge_kernel,
        grid=(T // tm,),
        in_specs=[
            row(D), row(D), row(D), row(2 * D),
            _resident((None, D, D), lambda i: (l, 0, 0)),
            _resident((None, D, D), lambda i: (l, 0, 0)),
            pl.BlockSpec((None, None, 1, mod4.shape[3]), lambda i: (l, i // spb, 0, 0)),
            pl.BlockSpec((None, 1, D), lambda i: (l, 0, 0)),
        ],
        out_specs=[row(D), row(D)],
        out_shape=[jax.ShapeDtypeStruct((T, D), F32), jax.ShapeDtypeStruct((T, D), F32)],
        compiler_params=_cparams(48, ("arbitrary",)),
        name="merge_out_projection",
    )(xt, attn_o, conv_o, gates, w_o_attn_bf, w_out_bf, mod4, norm_ffn_g)


def _topk_rows(s, payload, k):
    R, n = s.shape
    rid = lax.broadcasted_iota(jnp.int32, (R, n), 0).astype(F32)
    vals, pays = [], []
    for _ in range(k):
        m = jnp.max(s, axis=0, keepdims=True)
        pos = jnp.min(jnp.where(s == m, rid, float(R)), axis=0, keepdims=True)
        sel = rid == pos
        if payload is None:
            pays.append(pos)
        else:
            pays.append(jnp.max(jnp.where(sel, payload, -1.0), axis=0, keepdims=True))
        vals.append(m)
        s = jnp.where(sel, -jnp.inf, s)
    return jnp.concatenate(vals, axis=0), jnp.concatenate(pays, axis=0)


_CANDS = [(a, b) for a in range(TOPK) for b in range(TOPK) if (a + 1) * (b + 1) <= TOPK]


def _cand_rows(v0, v1):
    n_pad = -len(_CANDS) % SUBLANES
    ra = jnp.concatenate([v0[a:a + 1, :] for a, _ in _CANDS] + [v0[0:1, :]] * n_pad, axis=0)
    rb = jnp.concatenate([v1[b:b + 1, :] for _, b in _CANDS] + [v1[0:1, :]] * n_pad, axis=0)
    row = lax.broadcasted_iota(jnp.int32, ra.shape, 0)
    return ra, rb, row < len(_CANDS)


def _route_kernel(h_ref, wq_ref, keys_ref, off_ref, par_ref, gate_ref):
    q = _dot(h_ref[...].astype(BF16), wq_ref[...])
    ids_all, gates_all = [], []
    for h in range(N_RET_HEADS):
        half = []
        for p in range(2):
            hp = 2 * h + p
            qs = q[:, hp * N_KEYS:(hp + 1) * N_KEYS].astype(BF16)
            st = _dot_nt(keys_ref[hp], qs)
            half.append(_topk_rows(st, None, TOPK))
        (v0, i0), (v1, i1) = half
        sa, sb, real = _cand_rows(v0, v1)
        ia, ib, _ = _cand_rows(i0, i1)
        cand_s = jnp.where(real, sa + sb, -jnp.inf)
        cand_i = ia * float(N_KEYS) + ib
        top_s, top_i = _topk_rows(cand_s, cand_i, TOPK)
        e = jnp.exp(top_s - top_s[0:1, :])
        gates_all.append(e / jnp.sum(e, axis=0, keepdims=True))
        ids_all.append(top_i)
    ids = jnp.concatenate(ids_all, axis=0).T.astype(jnp.int32)
    gate_ref[...] = jnp.concatenate(gates_all, axis=0).T
    off_ref[...] = lax.shift_right_logical(ids, 1) * PAIR_WORD_ROWS
    par_ref[...] = ids & 1


def _route(h2, w_query_bf, keys_bf, l):
    T, D = h2.shape
    tr = 128
    nq = w_query_bf.shape[2]
    out = lambda dt: jax.ShapeDtypeStruct((T, N_PICKS), dt)
    return pl.pallas_call(
        _route_kernel,
        grid=(T // tr,),
        in_specs=[
            pl.BlockSpec((tr, D), lambda i: (i, 0)),
            _resident((None, D, nq), lambda i: (l, 0, 0)),
            _resident((None, 2 * N_RET_HEADS, N_KEYS, N_KEYS), lambda i: (l, 0, 0, 0)),
        ],
        out_specs=[pl.BlockSpec((tr, N_PICKS), lambda i: (i, 0))] * 3,
        out_shape=[out(jnp.int32), out(jnp.int32), out(F32)],
        compiler_params=_cparams(32, ("arbitrary",)),
        name="peer_route",
    )(h2, w_query_bf, keys_bf)


TOKENS_PER_ITER = 16


def _bf16_bits(v):
    b = pltpu.bitcast(v, jnp.uint32)
    return (b + jnp.uint32(0x7FFF) + ((b >> 16) & jnp.uint32(1))) >> 16


def _pack_kernel(x_ref, o_ref):
    n_rows = x_ref.shape[0]
    words_per_expert = CHUNKS // 2
    for j in range(words_per_expert):
        lo = _bf16_bits(x_ref[:, (2 * j) * LANES:(2 * j + 1) * LANES])
        hi = _bf16_bits(x_ref[:, (2 * j + 1) * LANES:(2 * j + 2) * LANES])
        o_ref[pl.ds(j, n_rows, stride=words_per_expert), :] = lo | (hi << 16)


def _pack_table(tab):
    depth, n_exp, d = tab.shape
    rows_per_step = 512
    out_rows = rows_per_step * d // (2 * LANES)
    return pl.pallas_call(
        _pack_kernel,
        grid=(depth, n_exp // rows_per_step),
        in_specs=[pl.BlockSpec((None, rows_per_step, d), lambda l, i: (l, i, 0))],
        out_specs=pl.BlockSpec((None, out_rows, LANES), lambda l, i: (l, i, 0)),
        out_shape=jax.ShapeDtypeStruct((depth, n_exp * d // (2 * LANES), LANES), jnp.uint32),
        compiler_params=_cparams(32, ("arbitrary", "arbitrary")),
        name="pack_expert_table",
    )(tab)


def _pair_vregs(tab_ref, off_ref, t, first, count):
    words = jnp.concatenate(
        [tab_ref[pl.ds(pl.multiple_of(off_ref[t, first + i], PAIR_WORD_ROWS), PAIR_WORD_ROWS), :]
         for i in range(count)], axis=0)
    return pltpu.bitcast(words, BF16)


def _tile_weights(tab_ref, off_ref, t, tile):
    base = tile * PICKS_PER_TILE
    wa = _pair_vregs(tab_ref, off_ref, t, base, HALF_TILE)
    wb = _pair_vregs(tab_ref, off_ref, t, base + HALF_TILE, HALF_TILE)
    return jnp.concatenate([wa, wb], axis=1)


def _token_loop(n_tokens, token):
    def body(i, carry):
        for r in range(TOKENS_PER_ITER):
            token(i * TOKENS_PER_ITER + r)
        return carry

    lax.fori_loop(0, n_tokens // TOKENS_PER_ITER, body, 0)


def _peer1_kernel(pair_ref, par_ref, gate_ref, h_ref, tab_ref, ge_ref, go_ref, coef_ref, u_scr):
    tq = h_ref.shape[0]
    tile_w = 2 * HALF_TILE * PAIR_ROWS
    half_w = HALF_TILE * PAIR_ROWS
    sub = lax.broadcasted_iota(jnp.int32, (CHUNKS, half_w), 0)
    lane = lax.broadcasted_iota(jnp.int32, (CHUNKS, half_w), 1)
    diag = (lane % CHUNKS) == sub
    zero = jnp.zeros((CHUNKS, LANES), BF16)

    def token(t):
        xhi, xlo = _split_bf16(h_ref[t])
        lhs = jnp.concatenate([
            jnp.concatenate([xhi, zero], axis=1), jnp.concatenate([xlo, zero], axis=1),
            jnp.concatenate([zero, xhi], axis=1), jnp.concatenate([zero, xlo], axis=1)], axis=0)
        for tile in range(N_TILES):
            s = _dot_nt(lhs, _tile_weights(tab_ref, pair_ref, t, tile))
            sa = s[0:CHUNKS] + s[CHUNKS:2 * CHUNKS]
            sb = s[2 * CHUNKS:3 * CHUNKS] + s[3 * CHUNKS:4 * CHUNKS]
            ua = jnp.sum(jnp.where(diag, sa, 0.0), axis=0, keepdims=True)
            ub = jnp.sum(jnp.where(diag, sb, 0.0), axis=0, keepdims=True)
            u_scr[pl.ds(t, 1), tile * tile_w:tile * tile_w + half_w] = ua
            u_scr[pl.ds(t, 1), tile * tile_w + half_w:(tile + 1) * tile_w] = ub

    _token_loop(tq, token)
    uhi, ulo = _split_bf16(u_scr[...])
    a_even = _dot(uhi, ge_ref[...]) + _dot(ulo, ge_ref[...])
    a_odd = _dot(uhi, go_ref[...]) + _dot(ulo, go_ref[...])
    a = jnp.where(par_ref[...] == 1, a_odd, a_even)
    gelu = 0.5 * a * (1.0 + lax.erf(a * (2.0 ** -0.5)))
    coef_ref[...] = gate_ref[...] * gelu


def _peer2_kernel(pair_ref, par_ref, coef_ref, x_ref, g2_ref, tab_ref, ee_ref, eo_ref, o_ref, c_scr):
    tq = x_ref.shape[0]
    half_w = HALF_TILE * PAIR_ROWS
    tile_w = 2 * half_w
    c = coef_ref[...]
    odd = par_ref[...] == 1
    cehi, celo = _split_bf16(jnp.where(odd, 0.0, c))
    cohi, colo = _split_bf16(jnp.where(odd, c, 0.0))
    c_scr[...] = (_dot(cehi, ee_ref[...]) + _dot(celo, ee_ref[...])
                  + _dot(cohi, eo_ref[...]) + _dot(colo, eo_ref[...]))
    sub = lax.broadcasted_iota(jnp.int32, (PAIR_ROWS, half_w), 0)
    lane = lax.broadcasted_iota(jnp.int32, (PAIR_ROWS, half_w), 1)
    diag = (lane % PAIR_ROWS) == sub
    g2 = g2_ref[...]

    def token(t):
        acc = jnp.zeros((4 * PAIR_ROWS, 2 * LANES), F32)
        for tile in range(N_TILES):
            parts = []
            for half in range(2):
                r = c_scr[pl.ds(t, 1), tile * tile_w + half * half_w:tile * tile_w + (half + 1) * half_w]
                lmat = jnp.where(diag, jnp.broadcast_to(r, (PAIR_ROWS, half_w)), 0.0)
                parts.extend(_split_bf16(lmat))
            lhs = jnp.concatenate(parts, axis=0)
            acc = acc + _dot(lhs, _tile_weights(tab_ref, pair_ref, t, tile))
        y16 = (acc[0:PAIR_ROWS, 0:LANES] + acc[PAIR_ROWS:2 * PAIR_ROWS, 0:LANES]
               + acc[2 * PAIR_ROWS:3 * PAIR_ROWS, LANES:] + acc[3 * PAIR_ROWS:, LANES:])
        y = y16[0:CHUNKS] + y16[CHUNKS:]
        o_ref[t] = x_ref[t] + g2 * y

    _token_loop(tq, token)


def _peer_constants():
    n = N_PICKS * PAIR_ROWS
    col = np.arange(n)
    pick = col // PAIR_ROWS
    upper = (col % PAIR_ROWS) >= CHUNKS
    sel = pick[:, None] == np.arange(N_PICKS)[None, :]
    ge = (sel & ~upper[:, None]).astype(np.float32)
    go = (sel & upper[:, None]).astype(np.float32)
    as_bf = lambda a: jnp.asarray(a, dtype=BF16)
    return as_bf(ge), as_bf(go), as_bf(ge.T), as_bf(go.T)


def _peer_tokens_per_step():
    return 256


def _peer1(pair, par, gate, h3, tab, ge, go, l):
    T = pair.shape[0]
    tq = _peer_tokens_per_step()
    n = N_PICKS * PAIR_ROWS
    row = lambda: pl.BlockSpec((tq, N_PICKS), lambda i: (i, 0))
    return pl.pallas_call(
        _peer1_kernel,
        grid=(T // tq,),
        in_specs=[
            pl.BlockSpec((tq, N_PICKS), lambda i: (i, 0), memory_space=pltpu.SMEM),
            row(), row(),
            pl.BlockSpec((tq, CHUNKS, LANES), lambda i: (i, 0, 0)),
            _resident((None,) + tab.shape[1:], lambda i: (l, 0, 0)),
            _resident((n, N_PICKS), lambda i: (0, 0)),
            _resident((n, N_PICKS), lambda i: (0, 0)),
        ],
        out_specs=row(),
        out_shape=jax.ShapeDtypeStruct((T, N_PICKS), F32),
        scratch_shapes=[pltpu.VMEM((tq, n), F32)],
        compiler_params=_cparams(48, ("arbitrary",)),
        name="peer_scores",
    )(pair, par, gate, h3, tab, ge, go)


def _peer2(pair, par, coef, x3, g2, tab, ee, eo, l, S):
    T = pair.shape[0]
    tq = _peer_tokens_per_step()
    n = N_PICKS * PAIR_ROWS
    spb = S // tq
    row = lambda: pl.BlockSpec((tq, N_PICKS), lambda i: (i, 0))
    tok = lambda: pl.BlockSpec((tq, CHUNKS, LANES), lambda i: (i, 0, 0))
    return pl.pallas_call(
        _peer2_kernel,
        grid=(T // tq,),
        in_specs=[
            pl.BlockSpec((tq, N_PICKS), lambda i: (i, 0), memory_space=pltpu.SMEM),
            row(), row(), tok(),
            pl.BlockSpec((None, CHUNKS, LANES), lambda i: (i // spb, 0, 0)),
            _resident((None,) + tab.shape[1:], lambda i: (l, 0, 0)),
            _resident((N_PICKS, n), lambda i: (0, 0)),
            _resident((N_PICKS, n), lambda i: (0, 0)),
        ],
        out_specs=tok(),
        out_shape=jax.ShapeDtypeStruct((T, CHUNKS, LANES), F32),
        scratch_shapes=[pltpu.VMEM((tq, n), F32)],
        compiler_params=_cparams(48, ("arbitrary",)),
        name="peer_combine",
    )(pair, par, coef, x3, g2, tab, ee, eo)


def kernel(x, c, positions, ada_w, ada_b, norm_mix_g, w_in, q_norm_g, k_norm_g, attn_sinks,
           w_o_attn, conv_w, conv_b, conv_ln_g, conv_ln_b, w_o_conv, w_out, norm_ffn_g,
           peer_w_query, peer_sub_keys, peer_u, peer_v):
    B, S, D = x.shape
    depth = ada_w.shape[0]
    T = B * S
    assert D == CHUNKS * LANES and S % 512 == 0

    cos128, sin128 = _rope_tables(positions)
    mod = _modulation(c, ada_w, ada_b)
    mod4 = mod.reshape(depth, B, 1, 6 * D)
    g2_all = mod[:, :, 5 * D:6 * D].reshape(depth, B, CHUNKS, LANES)

    w_in_bf = w_in.astype(BF16)
    w_o_attn_bf = w_o_attn.astype(BF16)
    w_o_conv_bf = w_o_conv.astype(BF16)
    w_out_bf = w_out.astype(BF16)
    w_query_bf = peer_w_query.astype(BF16)
    keys_bf = peer_sub_keys.astype(BF16).reshape(depth, 2 * N_RET_HEADS, N_KEYS, peer_sub_keys.shape[-1])
    u_bf = _pack_table(peer_u)
    v_bf = _pack_table(peer_v)
    ge, go, ee, eo = _peer_constants()
    vec3 = lambda a: a.reshape(depth, 1, a.shape[-1])
    gq128 = jnp.concatenate([q_norm_g, q_norm_g], axis=-1).reshape(depth, 1, LANES)
    gk128 = jnp.concatenate([k_norm_g, k_norm_g], axis=-1).reshape(depth, 1, LANES)
    norm_mix3, norm_ffn3 = vec3(norm_mix_g), vec3(norm_ffn_g)
    conv_b3, ln_g3, ln_b3 = vec3(conv_b), vec3(conv_ln_g), vec3(conv_ln_b)

    xt = x.reshape(T, D)
    for l in range(depth):
        q, kv, glu, gates = _in_projection(xt, mod4, norm_mix3, w_in_bf, l, S)
        attn_o = _attention(q, kv, cos128, sin128, gq128[l], gk128[l], attn_sinks[l], B, S)
        conv_o = _conv_branch(glu, conv_w, conv_b3, ln_g3, ln_b3, w_o_conv_bf, l, B, S)
        x_mid, h2 = _merge(xt, attn_o, conv_o, gates, w_o_attn_bf, w_out_bf, mod4, norm_ffn3, l, S)
        pair, par, gate = _route(h2, w_query_bf, keys_bf, l)
        coef = _peer1(pair, par, gate, h2.reshape(T, CHUNKS, LANES), u_bf, ge, go, l)
        x3 = _peer2(pair, par, coef, x_mid.reshape(T, CHUNKS, LANES), g2_all[l], v_bf, ee, eo, l, S)
        xt = x3.reshape(T, D)
    return xt.reshape(B, S, D)
```

```python
import numpy as np
import jax
import jax.numpy as jnp
from jax import lax
from jax.experimental import pallas as pl
from jax.experimental.pallas import tpu as pltpu

F32 = jnp.float32
BF16 = jnp.bfloat16

HEAD_DIM = 64
N_Q_HEADS = 16
N_KV_HEADS = 4
Q_PER_KV = N_Q_HEADS // N_KV_HEADS
WINDOW = 128
ROPE_THETA = 10000.0
CONV_WIDTH = 31
N_KEYS = 128
N_RET_HEADS = 8
TOPK = 16
N_PICKS = N_RET_HEADS * TOPK
EPS = 1e-6
NEG_INF = -1e30

LANES = 128
SUBLANES = 8
CHUNKS = 8
EXPERT_WORD_ROWS = CHUNKS // 2
MXU_TILE = 256
HALF_TILE = MXU_TILE // CHUNKS
PICKS_PER_TILE = 2 * HALF_TILE
N_TILES = N_PICKS // PICKS_PER_TILE
PICK_LANES = N_PICKS * CHUNKS


def _cparams(vmem_mb, sem):
    return pltpu.CompilerParams(dimension_semantics=sem, vmem_limit_bytes=vmem_mb * 1024 * 1024)


def _resident(block_shape, index_map):
    return pl.BlockSpec(block_shape, index_map, pipeline_mode=pl.Buffered(1))


def _split_bf16(v):
    hi = v.astype(BF16)
    lo = (v - hi.astype(F32)).astype(BF16)
    return hi, lo


def _dot(a, b):
    return jnp.dot(a, b, preferred_element_type=F32)


def _dot_nt(a, b):
    return lax.dot_general(a, b, (((1,), (1,)), ((), ())), preferred_element_type=F32)


def _rope_kernel(pos_ref, inv_ref, cs_ref):
    ang = inv_ref[...] * pos_ref[...].astype(F32)
    cs_ref[0:32, :] = jnp.cos(ang)
    cs_ref[32:64, :] = jnp.sin(ang)


def _rope_tables(positions):
    T = positions.size
    tb = min(2048, T)
    inv = ROPE_THETA ** (-jnp.arange(0, HEAD_DIM, 2, dtype=F32) / HEAD_DIM)
    cs = pl.pallas_call(
        _rope_kernel,
        grid=(T // tb,),
        in_specs=[pl.BlockSpec((1, tb), lambda i: (0, i)), pl.BlockSpec((32, 1), lambda i: (0, 0))],
        out_specs=pl.BlockSpec((64, tb), lambda i: (0, i)),
        out_shape=jax.ShapeDtypeStruct((64, T), F32),
        compiler_params=_cparams(32, ("arbitrary",)),
        name="rope_tables",
    )(positions.reshape(1, T), inv.reshape(32, 1))
    cos = cs[0:32].T
    sin = cs[32:64].T
    cos128 = jnp.concatenate([cos, cos, cos, cos], axis=1)
    sin128 = jnp.concatenate([-sin, sin, -sin, sin], axis=1)
    return cos128, sin128


def _mod_kernel(c_ref, w_ref, b_ref, o_ref):
    c = c_ref[...]
    ca = c * jax.nn.sigmoid(c)
    o_ref[...] = jnp.dot(ca, w_ref[...], preferred_element_type=F32,
                         precision=lax.Precision.HIGHEST) + b_ref[...]


def _modulation(c, ada_w, ada_b):
    depth, D, six_d = ada_w.shape
    B = c.shape[0]
    tn = 1024
    return pl.pallas_call(
        _mod_kernel,
        grid=(depth, six_d // tn),
        in_specs=[
            pl.BlockSpec((B, D), lambda l, j: (0, 0)),
            pl.BlockSpec((None, D, tn), lambda l, j: (l, 0, j)),
            pl.BlockSpec((None, 1, tn), lambda l, j: (l, 0, j)),
        ],
        out_specs=pl.BlockSpec((None, B, tn), lambda l, j: (l, 0, j)),
        out_shape=jax.ShapeDtypeStruct((depth, B, six_d), F32),
        compiler_params=_cparams(32, ("arbitrary", "arbitrary")),
        name="adaln_modulation",
    )(c, ada_w, ada_b.reshape(depth, 1, six_d))


def _in_kernel(x_ref, mod_ref, g_ref, w_ref, q_ref, kv_ref, glu_ref, gates_ref):
    D = x_ref.shape[1]
    x = x_ref[...]
    y = x * lax.rsqrt(jnp.mean(x * x, axis=-1, keepdims=True) + EPS) * g_ref[...]
    h = (y * (1.0 + mod_ref[:, D:2 * D]) + mod_ref[:, 0:D]).astype(BF16)
    c0 = 0
    for ref in (q_ref, kv_ref, glu_ref, gates_ref):
        n = ref.shape[1]
        ref[...] = _dot(h, w_ref[:, c0:c0 + n])
        c0 += n


def _in_projection(xt, mod4, norm_g, w_in_bf, l, S):
    T, D = xt.shape
    n_in = w_in_bf.shape[2]
    tm = 256
    q_w = N_Q_HEADS * HEAD_DIM
    kv_w = 2 * N_KV_HEADS * HEAD_DIM
    rest = (n_in - q_w - kv_w) // 2
    spb = S // tm
    widths = (q_w, kv_w, rest, rest)
    return pl.pallas_call(
        _in_kernel,
        grid=(T // tm,),
        in_specs=[
            pl.BlockSpec((tm, D), lambda i: (i, 0)),
            pl.BlockSpec((None, None, 1, mod4.shape[3]), lambda i: (l, i // spb, 0, 0)),
            pl.BlockSpec((None, 1, D), lambda i: (l, 0, 0)),
            _resident((None, D, n_in), lambda i: (l, 0, 0)),
        ],
        out_specs=[pl.BlockSpec((tm, w), lambda i: (i, 0)) for w in widths],
        out_shape=[jax.ShapeDtypeStruct((T, w), F32) for w in widths],
        compiler_params=_cparams(48, ("arbitrary",)),
        name="in_projection",
    )(xt, mod4, norm_g, w_in_bf)


def _norm_rope_chunk(raw, g, cos, sin, lane):
    sq = raw * raw
    low = lane < HEAD_DIM
    ss0 = jnp.sum(jnp.where(low, sq, 0.0), axis=-1, keepdims=True)
    ss1 = jnp.sum(jnp.where(low, 0.0, sq), axis=-1, keepdims=True)
    scale = jnp.where(low, lax.rsqrt(ss0 / HEAD_DIM + EPS), lax.rsqrt(ss1 / HEAD_DIM + EPS))
    xg = raw * scale * g
    first = (lane % HEAD_DIM) < (HEAD_DIM // 2)
    rot = jnp.where(first, pltpu.roll(xg, LANES - HEAD_DIM // 2, 1), pltpu.roll(xg, HEAD_DIM // 2, 1))
    return xg * cos + rot * sin


def _attn_kernel(sink_ref, q_ref, kvc_ref, kvp_ref, cosc_ref, sinc_ref, cosp_ref, sinp_ref,
                 gq_ref, gk_ref, o_ref, kd_scr, vd_scr, q_scr, s_scr, p_scr):
    tq = q_ref.shape[0]
    blk = WINDOW
    j = pl.program_id(1)
    kvw = N_KV_HEADS * HEAD_DIM
    gq = gq_ref[...]
    gk = gk_ref[...]

    for (src, cos_r, sin_r, r0, n) in ((kvp_ref, cosp_ref, sinp_ref, 0, blk),
                                       (kvc_ref, cosc_ref, sinc_ref, blk, tq)):
        lane = lax.broadcasted_iota(jnp.int32, (n, LANES), 1)
        low = lane < HEAD_DIM
        cos = cos_r[...]
        sin = sin_r[...]
        for c in range(kvw // LANES):
            kc = _norm_rope_chunk(src[:, c * LANES:(c + 1) * LANES], gk, cos, sin, lane)
            vc = src[:, kvw + c * LANES:kvw + (c + 1) * LANES]
            kc_sw = pltpu.roll(kc, HEAD_DIM, 1)
            vc_sw = pltpu.roll(vc, HEAD_DIM, 1)
            kd_scr[2 * c, r0:r0 + n, :] = jnp.where(low, kc, kc_sw).astype(BF16)
            kd_scr[2 * c + 1, r0:r0 + n, :] = jnp.where(low, kc_sw, kc).astype(BF16)
            vd_scr[2 * c, r0:r0 + n, :] = jnp.where(low, vc, vc_sw).astype(BF16)
            vd_scr[2 * c + 1, r0:r0 + n, :] = jnp.where(low, vc_sw, vc).astype(BF16)

    scale = HEAD_DIM ** -0.5
    lane_t = lax.broadcasted_iota(jnp.int32, (tq, LANES), 1)
    low_t = lane_t < HEAD_DIM
    for c in range(N_Q_HEADS // 2):
        qc = _norm_rope_chunk(q_ref[:, c * LANES:(c + 1) * LANES], gq, cosc_ref[...], sinc_ref[...],
                              lane_t) * scale
        q_scr[2 * c] = jnp.where(low_t, qc, 0.0).astype(BF16)
        q_scr[2 * c + 1] = jnp.where(low_t, 0.0, qc).astype(BF16)

    low_q = lax.broadcasted_iota(jnp.int32, (blk, LANES), 1) < HEAD_DIM
    qi = lax.broadcasted_iota(jnp.int32, (blk, 2 * blk), 0)
    kj = lax.broadcasted_iota(jnp.int32, (blk, 2 * blk), 1)
    diff = qi + blk - kj
    in_window = (diff >= 0) & (diff < WINDOW)

    for n in range(tq // blk):
        rows = slice(n * blk, (n + 1) * blk)
        keys = slice(n * blk, (n + 2) * blk)
        valid = in_window
        if n == 0:
            valid = valid & (kj >= jnp.where(j > 0, 0, blk))
        for g in range(N_KV_HEADS):
            q4 = jnp.concatenate([q_scr[g * Q_PER_KV + r, rows, :] for r in range(Q_PER_KV)], axis=0)
            s_scr[g] = _dot_nt(q4, kd_scr[g, keys, :])
        for g in range(N_KV_HEADS):
            for r in range(Q_PER_KV):
                hrows = slice(r * blk, (r + 1) * blk)
                s = jnp.where(valid, s_scr[g, hrows, :], NEG_INF)
                sink = sink_ref[g * Q_PER_KV + r]
                m = jnp.maximum(jnp.max(s, axis=-1, keepdims=True), sink)
                p = jnp.exp(s - m)
                p = p / (jnp.sum(p, axis=-1, keepdims=True) + jnp.exp(sink - m))
                p_scr[g, hrows, :] = p.astype(BF16)
        for g in range(N_KV_HEADS):
            o4 = _dot(p_scr[g], vd_scr[g, keys, :])
            for r2 in range(Q_PER_KV // 2):
                c = (g * Q_PER_KV) // 2 + r2
                oe = o4[(2 * r2) * blk:(2 * r2 + 1) * blk, :]
                oo = o4[(2 * r2 + 1) * blk:(2 * r2 + 2) * blk, :]
                o_ref[rows, c * LANES:(c + 1) * LANES] = jnp.where(low_q, oe, oo).astype(BF16)


def _attention(q, kv, cos128, sin128, gq, gk, sinks, B, S):
    T = q.shape[0]
    tq = 512
    blk = WINDOW
    nq = S // tq
    bpq = tq // blk
    cur = lambda b, j: (b * nq + j, 0)
    prev = lambda b, j: (jnp.maximum((b * nq + j) * bpq - 1, 0), 0)
    kvw = kv.shape[1]
    return pl.pallas_call(
        _attn_kernel,
        grid=(B, nq),
        in_specs=[
            pl.BlockSpec(memory_space=pltpu.SMEM),
            pl.BlockSpec((tq, q.shape[1]), cur),
            pl.BlockSpec((tq, kvw), cur),
            pl.BlockSpec((blk, kvw), prev),
            pl.BlockSpec((tq, LANES), cur),
            pl.BlockSpec((tq, LANES), cur),
            pl.BlockSpec((blk, LANES), prev),
            pl.BlockSpec((blk, LANES), prev),
            pl.BlockSpec((1, LANES), lambda b, j: (0, 0)),
            pl.BlockSpec((1, LANES), lambda b, j: (0, 0)),
        ],
        out_specs=pl.BlockSpec((tq, q.shape[1]), cur),
        scratch_shapes=[pltpu.VMEM((N_KV_HEADS, tq + blk, LANES), BF16),
                        pltpu.VMEM((N_KV_HEADS, tq + blk, LANES), BF16),
                        pltpu.VMEM((N_Q_HEADS, tq, LANES), BF16),
                        pltpu.VMEM((N_KV_HEADS, Q_PER_KV * blk, 2 * blk), F32),
                        pltpu.VMEM((N_KV_HEADS, Q_PER_KV * blk, 2 * blk), BF16)],
        out_shape=jax.ShapeDtypeStruct((T, q.shape[1]), BF16),
        compiler_params=_cparams(48, ("arbitrary", "arbitrary")),
        name="swa_attention",
    )(sinks, q, kv, kv, cos128, sin128, cos128, sin128, gq, gk)


HALO = 32


def _conv_kernel(glu_ref, cw_ref, cb_ref, lg_ref, lb_ref, wo_ref, o_ref, ubuf):
    tc = glu_ref.shape[0]
    C = o_ref.shape[1]
    j = pl.program_id(1)

    @pl.when(j == 0)
    def _():
        ubuf[0:HALO, :] = jnp.zeros((HALO, C), F32)
        ubuf[HALO + tc:, :] = jnp.zeros((SUBLANES, C), F32)

    @pl.when(j > 0)
    def _():
        ubuf[0:HALO, :] = ubuf[tc:tc + HALO, :]

    ubuf[HALO:HALO + tc, :] = glu_ref[:, 0:C] * jax.nn.sigmoid(glu_ref[:, C:2 * C])
    off = HALO - (CONV_WIDTH - 1)
    acc = cb_ref[...]
    for rem in range(SUBLANES):
        part = None
        for k in range(CONV_WIDTH):
            if (off + k) % SUBLANES == rem:
                base = off + k - rem
                term = cw_ref[k:k + 1, :] * ubuf[base:base + tc + SUBLANES, :]
                part = term if part is None else part + term
        acc = acc + part[rem:rem + tc, :]
    mu = jnp.mean(acc, axis=-1, keepdims=True)
    d = acc - mu
    var = jnp.mean(d * d, axis=-1, keepdims=True)
    yn = d * lax.rsqrt(var + EPS) * lg_ref[...] + lb_ref[...]
    act = yn * jax.nn.sigmoid(yn)
    o_ref[...] = _dot(act.astype(BF16), wo_ref[...])


def _conv_branch(glu, conv_w, conv_b, ln_g, ln_b, w_o_conv_bf, l, B, S):
    T = glu.shape[0]
    C = glu.shape[1] // 2
    tc = 256
    nc = S // tc
    vec = lambda: pl.BlockSpec((None, 1, C), lambda b, j: (l, 0, 0))
    return pl.pallas_call(
        _conv_kernel,
        grid=(B, nc),
        in_specs=[
            pl.BlockSpec((tc, 2 * C), lambda b, j: (b * nc + j, 0)),
            pl.BlockSpec((None, CONV_WIDTH, C), lambda b, j: (l, 0, 0)),
            vec(), vec(), vec(),
            _resident((None, C, C), lambda b, j: (l, 0, 0)),
        ],
        out_specs=pl.BlockSpec((tc, C), lambda b, j: (b * nc + j, 0)),
        out_shape=jax.ShapeDtypeStruct((T, C), F32),
        scratch_shapes=[pltpu.VMEM((tc + HALO + SUBLANES, C), F32)],
        compiler_params=_cparams(48, ("arbitrary", "arbitrary")),
        name="conv_branch",
    )(glu, conv_w, conv_b, ln_g, ln_b, w_o_conv_bf)


def _merge_kernel(x_ref, ao_ref, co_ref, gates_ref, woa_ref, wout_ref, mod_ref, gf_ref,
                  xmid_ref, h2_ref):
    D = x_ref.shape[1]
    attn = _dot(ao_ref[...], woa_ref[...])
    merged = (jax.nn.sigmoid(gates_ref[:, 0:D]) * attn
              + jax.nn.sigmoid(gates_ref[:, D:2 * D]) * co_ref[...])
    out = _dot(merged.astype(BF16), wout_ref[...])
    xm = x_ref[...] + mod_ref[:, 2 * D:3 * D] * out
    xmid_ref[...] = xm
    y = xm * lax.rsqrt(jnp.mean(xm * xm, axis=-1, keepdims=True) + EPS) * gf_ref[...]
    h2_ref[...] = y * (1.0 + mod_ref[:, 4 * D:5 * D]) + mod_ref[:, 3 * D:4 * D]


def _merge(xt, attn_o, conv_o, gates, w_o_attn_bf, w_out_bf, mod4, norm_ffn_g, l, S):
    T, D = xt.shape
    tm = 256
    spb = S // tm
    row = lambda w: pl.BlockSpec((tm, w), lambda i: (i, 0))
    return pl.pallas_call(
        _merge_kernel,
        grid=(T // tm,),
        in_specs=[
            row(D), row(D), row(D), row(2 * D),
            _resident((None, D, D), lambda i: (l, 0, 0)),
            _resident((None, D, D), lambda i: (l, 0, 0)),
            pl.BlockSpec((None, None, 1, mod4.shape[3]), lambda i: (l, i // spb, 0, 0)),
            pl.BlockSpec((None, 1, D), lambda i: (l, 0, 0)),
        ],
        out_specs=[row(D), row(D)],
        out_shape=[jax.ShapeDtypeStruct((T, D), F32), jax.ShapeDtypeStruct((T, D), F32)],
        compiler_params=_cparams(48, ("arbitrary",)),
        name="merge_out_projection",
    )(xt, attn_o, conv_o, gates, w_o_attn_bf, w_out_bf, mod4, norm_ffn_g)


def _topk_rows(s, payload, k):
    R, n = s.shape
    rid = lax.broadcasted_iota(jnp.int32, (R, n), 0).astype(F32)
    vals, pays = [], []
    for _ in range(k):
        m = jnp.max(s, axis=0, keepdims=True)
        pos = jnp.min(jnp.where(s == m, rid, float(R)), axis=0, keepdims=True)
        sel = rid == pos
        if payload is None:
            pays.append(pos)
        else:
            pays.append(jnp.max(jnp.where(sel, payload, -1.0), axis=0, keepdims=True))
        vals.append(m)
        s = jnp.where(sel, -jnp.inf, s)
    return jnp.concatenate(vals, axis=0), jnp.concatenate(pays, axis=0)


_CANDS = [(a, b) for a in range(TOPK) for b in range(TOPK) if (a + 1) * (b + 1) <= TOPK]


def _cand_rows(v0, v1):
    n_pad = -len(_CANDS) % SUBLANES
    ra = jnp.concatenate([v0[a:a + 1, :] for a, _ in _CANDS] + [v0[0:1, :]] * n_pad, axis=0)
    rb = jnp.concatenate([v1[b:b + 1, :] for _, b in _CANDS] + [v1[0:1, :]] * n_pad, axis=0)
    row = lax.broadcasted_iota(jnp.int32, ra.shape, 0)
    return ra, rb, row < len(_CANDS)


def _route_kernel(h_ref, wq_ref, keys_ref, off_ref, gate_ref):
    q = _dot(h_ref[...].astype(BF16), wq_ref[...])
    ids_all, gates_all = [], []
    for h in range(N_RET_HEADS):
        half = []
        for p in range(2):
            hp = 2 * h + p
            qs = q[:, hp * N_KEYS:(hp + 1) * N_KEYS].astype(BF16)
            st = _dot_nt(keys_ref[hp], qs)
            half.append(_topk_rows(st, None, TOPK))
        (v0, i0), (v1, i1) = half
        sa, sb, real = _cand_rows(v0, v1)
        ia, ib, _ = _cand_rows(i0, i1)
        cand_s = jnp.where(real, sa + sb, -jnp.inf)
        cand_i = ia * float(N_KEYS) + ib
        top_s, top_i = _topk_rows(cand_s, cand_i, TOPK)
        e = jnp.exp(top_s - top_s[0:1, :])
        gates_all.append(e / jnp.sum(e, axis=0, keepdims=True))
        ids_all.append(top_i)
    ids = jnp.concatenate(ids_all, axis=0).T.astype(jnp.int32)
    gate_ref[...] = jnp.concatenate(gates_all, axis=0).T
    off_ref[...] = ids * EXPERT_WORD_ROWS


def _route(h2, w_query_bf, keys_bf, l):
    T, D = h2.shape
    tr = 128
    nq = w_query_bf.shape[2]
    out = lambda dt: jax.ShapeDtypeStruct((T, N_PICKS), dt)
    return pl.pallas_call(
        _route_kernel,
        grid=(T // tr,),
        in_specs=[
            pl.BlockSpec((tr, D), lambda i: (i, 0)),
            _resident((None, D, nq), lambda i: (l, 0, 0)),
            _resident((None, 2 * N_RET_HEADS, N_KEYS, N_KEYS), lambda i: (l, 0, 0, 0)),
        ],
        out_specs=[pl.BlockSpec((tr, N_PICKS), lambda i: (i, 0))] * 2,
        out_shape=[out(jnp.int32), out(F32)],
        compiler_params=_cparams(32, ("arbitrary",)),
        name="peer_route",
    )(h2, w_query_bf, keys_bf)


TOKENS_PER_ITER = 32


def _bf16_bits(v):
    b = pltpu.bitcast(v, jnp.uint32)
    return (b + jnp.uint32(0x7FFF) + ((b >> 16) & jnp.uint32(1))) >> 16


def _pack_kernel(x_ref, o_ref):
    n_rows = x_ref.shape[0]
    words_per_expert = CHUNKS // 2
    for j in range(words_per_expert):
        lo = _bf16_bits(x_ref[:, (2 * j) * LANES:(2 * j + 1) * LANES])
        hi = _bf16_bits(x_ref[:, (2 * j + 1) * LANES:(2 * j + 2) * LANES])
        o_ref[pl.ds(j, n_rows, stride=words_per_expert), :] = lo | (hi << 16)


def _pack_table(tab):
    depth, n_exp, d = tab.shape
    rows_per_step = 512
    out_rows = rows_per_step * d // (2 * LANES)
    return pl.pallas_call(
        _pack_kernel,
        grid=(depth, n_exp // rows_per_step),
        in_specs=[pl.BlockSpec((None, rows_per_step, d), lambda l, i: (l, i, 0))],
        out_specs=pl.BlockSpec((None, out_rows, LANES), lambda l, i: (l, i, 0)),
        out_shape=jax.ShapeDtypeStruct((depth, n_exp * d // (2 * LANES), LANES), jnp.uint32),
        compiler_params=_cparams(32, ("arbitrary", "arbitrary")),
        name="pack_expert_table",
    )(tab)


def _expert_rows(tab_ref, off_ref, t, first, count):
    words = jnp.concatenate(
        [tab_ref[pl.ds(pl.multiple_of(off_ref.at[:, pl.ds(first + i, 1)][t, 0], EXPERT_WORD_ROWS),
                       EXPERT_WORD_ROWS), :]
         for i in range(count)], axis=0)
    return pltpu.bitcast(words, BF16)


def _tile_weights(tab_ref, off_ref, t, tile):
    base = tile * PICKS_PER_TILE
    wa = _expert_rows(tab_ref, off_ref, t, base, HALF_TILE)
    wb = _expert_rows(tab_ref, off_ref, t, base + HALF_TILE, HALF_TILE)
    return jnp.concatenate([wa, wb], axis=1)


def _stage_offsets(off_hbm, off_smem, sem):
    tq = off_smem.shape[0] // 2
    i = pl.program_id(0)
    slot = lax.rem(i, 2)

    def copy(step, s):
        return pltpu.make_async_copy(off_hbm.at[pl.ds(step * tq, tq), :],
                                     off_smem.at[pl.ds(s * tq, tq), :], sem.at[s])

    @pl.when(i == 0)
    def _():
        copy(0, 0).start()

    @pl.when(i + 1 < pl.num_programs(0))
    def _():
        copy(i + 1, 1 - slot).start()

    copy(i, slot).wait()
    return slot * tq


def _token_loop(n_tokens, token):
    def body(i, carry):
        for r in range(TOKENS_PER_ITER):
            token(i * TOKENS_PER_ITER + r)
        return carry

    lax.fori_loop(0, n_tokens // TOKENS_PER_ITER, body, 0)


def _peer1_kernel(off_hbm, gate_ref, h_ref, tab_ref, g_ref, coef_ref, u_scr, off_ref, off_sem):
    tq = h_ref.shape[0]
    row0 = _stage_offsets(off_hbm, off_ref, off_sem)
    half_w = MXU_TILE
    tile_w = 2 * half_w
    sub = lax.broadcasted_iota(jnp.int32, (CHUNKS, half_w), 0)
    lane = lax.broadcasted_iota(jnp.int32, (CHUNKS, half_w), 1)
    diag = (lane % CHUNKS) == sub
    zero = jnp.zeros((CHUNKS, LANES), BF16)

    def token(t):
        xhi, xlo = _split_bf16(h_ref[t])
        lhs = jnp.concatenate([
            jnp.concatenate([xhi, zero], axis=1), jnp.concatenate([xlo, zero], axis=1),
            jnp.concatenate([zero, xhi], axis=1), jnp.concatenate([zero, xlo], axis=1)], axis=0)
        for tile in range(N_TILES):
            s = _dot_nt(lhs, _tile_weights(tab_ref, off_ref, row0 + t, tile))
            sa = s[0:CHUNKS] + s[CHUNKS:2 * CHUNKS]
            sb = s[2 * CHUNKS:3 * CHUNKS] + s[3 * CHUNKS:4 * CHUNKS]
            ua = jnp.sum(jnp.where(diag, sa, 0.0), axis=0, keepdims=True)
            ub = jnp.sum(jnp.where(diag, sb, 0.0), axis=0, keepdims=True)
            u_scr[pl.ds(t, 1), tile * tile_w:tile * tile_w + half_w] = ua
            u_scr[pl.ds(t, 1), tile * tile_w + half_w:(tile + 1) * tile_w] = ub

    _token_loop(tq, token)
    uhi, ulo = _split_bf16(u_scr[...])
    a = _dot(uhi, g_ref[...]) + _dot(ulo, g_ref[...])
    gelu = 0.5 * a * (1.0 + lax.erf(a * (2.0 ** -0.5)))
    coef_ref[...] = gate_ref[...] * gelu


def _peer2_kernel(off_hbm, coef_ref, x_ref, g2_ref, tab_ref, e_ref, o_ref, c_scr, off_ref, off_sem):
    tq = x_ref.shape[0]
    row0 = _stage_offsets(off_hbm, off_ref, off_sem)
    half_w = MXU_TILE
    tile_w = 2 * half_w
    chi, clo = _split_bf16(coef_ref[...])
    c_scr[...] = _dot(chi, e_ref[...]) + _dot(clo, e_ref[...])
    sub = lax.broadcasted_iota(jnp.int32, (2 * CHUNKS, half_w), 0)
    lane = lax.broadcasted_iota(jnp.int32, (2 * CHUNKS, half_w), 1)
    diag = (lane % CHUNKS) == (sub % CHUNKS)
    group_a = sub < CHUNKS
    g2 = g2_ref[...]

    def token(t):
        acc = jnp.zeros((4 * CHUNKS, 2 * LANES), F32)
        for tile in range(N_TILES):
            ra = c_scr[pl.ds(t, 1), tile * tile_w:tile * tile_w + half_w]
            rb = c_scr[pl.ds(t, 1), tile * tile_w + half_w:(tile + 1) * tile_w]
            both = jnp.where(group_a, jnp.broadcast_to(ra, diag.shape), jnp.broadcast_to(rb, diag.shape))
            lhi, llo = _split_bf16(jnp.where(diag, both, 0.0))
            lhs = jnp.concatenate([lhi, llo], axis=0)
            acc = acc + _dot(lhs, _tile_weights(tab_ref, off_ref, row0 + t, tile))
        y = (acc[0:CHUNKS, 0:LANES] + acc[2 * CHUNKS:3 * CHUNKS, 0:LANES]
             + acc[CHUNKS:2 * CHUNKS, LANES:] + acc[3 * CHUNKS:, LANES:])
        o_ref[t] = x_ref[t] + g2 * y

    _token_loop(tq, token)


def _peer_constants():
    sel = (np.arange(PICK_LANES)[:, None] // CHUNKS) == np.arange(N_PICKS)[None, :]
    g = jnp.asarray(sel.astype(np.float32), dtype=BF16)
    return g, g.T


def _peer_tokens_per_step():
    return 256


def _peer1(off, gate, h3, tab, g, l):
    T = off.shape[0]
    tq = _peer_tokens_per_step()
    row = lambda: pl.BlockSpec((tq, N_PICKS), lambda i: (i, 0))
    return pl.pallas_call(
        _peer1_kernel,
        grid=(T // tq,),
        in_specs=[
            pl.BlockSpec(memory_space=pl.ANY),
            row(),
            pl.BlockSpec((tq, CHUNKS, LANES), lambda i: (i, 0, 0)),
            _resident((None,) + tab.shape[1:], lambda i: (l, 0, 0)),
            _resident((PICK_LANES, N_PICKS), lambda i: (0, 0)),
        ],
        out_specs=row(),
        out_shape=jax.ShapeDtypeStruct((T, N_PICKS), F32),
        scratch_shapes=[pltpu.VMEM((tq, PICK_LANES), F32),
                        pltpu.SMEM((2 * tq, N_PICKS), jnp.int32),
                        pltpu.SemaphoreType.DMA((2,))],
        compiler_params=_cparams(48, ("arbitrary",)),
        name="peer_scores",
    )(off, gate, h3, tab, g)


def _peer2(off, coef, x3, g2, tab, e, l, S):
    T = off.shape[0]
    tq = _peer_tokens_per_step()
    spb = S // tq
    row = lambda: pl.BlockSpec((tq, N_PICKS), lambda i: (i, 0))
    tok = lambda: pl.BlockSpec((tq, CHUNKS, LANES), lambda i: (i, 0, 0))
    return pl.pallas_call(
        _peer2_kernel,
        grid=(T // tq,),
        in_specs=[
            pl.BlockSpec(memory_space=pl.ANY),
            row(), tok(),
            pl.BlockSpec((None, CHUNKS, LANES), lambda i: (i // spb, 0, 0)),
            _resident((None,) + tab.shape[1:], lambda i: (l, 0, 0)),
            _resident((N_PICKS, PICK_LANES), lambda i: (0, 0)),
        ],
        out_specs=tok(),
        out_shape=jax.ShapeDtypeStruct((T, CHUNKS, LANES), F32),
        scratch_shapes=[pltpu.VMEM((tq, PICK_LANES), F32),
                        pltpu.SMEM((2 * tq, N_PICKS), jnp.int32),
                        pltpu.SemaphoreType.DMA((2,))],
        compiler_params=_cparams(48, ("arbitrary",)),
        name="peer_combine",
    )(off, coef, x3, g2, tab, e)


def kernel(x, c, positions, ada_w, ada_b, norm_mix_g, w_in, q_norm_g, k_norm_g, attn_sinks,
           w_o_attn, conv_w, conv_b, conv_ln_g, conv_ln_b, w_o_conv, w_out, norm_ffn_g,
           peer_w_query, peer_sub_keys, peer_u, peer_v):
    B, S, D = x.shape
    depth = ada_w.shape[0]
    T = B * S
    assert D == CHUNKS * LANES and S % 512 == 0

    cos128, sin128 = _rope_tables(positions)
    mod = _modulation(c, ada_w, ada_b)
    mod4 = mod.reshape(depth, B, 1, 6 * D)
    g2_all = mod[:, :, 5 * D:6 * D].reshape(depth, B, CHUNKS, LANES)

    w_in_bf = w_in.astype(BF16)
    w_o_attn_bf = w_o_attn.astype(BF16)
    w_o_conv_bf = w_o_conv.astype(BF16)
    w_out_bf = w_out.astype(BF16)
    w_query_bf = peer_w_query.astype(BF16)
    keys_bf = peer_sub_keys.astype(BF16).reshape(depth, 2 * N_RET_HEADS, N_KEYS, peer_sub_keys.shape[-1])
    u_bf = _pack_table(peer_u)
    v_bf = _pack_table(peer_v)
    pick_sum, pick_spread = _peer_constants()
    vec3 = lambda a: a.reshape(depth, 1, a.shape[-1])
    gq128 = jnp.concatenate([q_norm_g, q_norm_g], axis=-1).reshape(depth, 1, LANES)
    gk128 = jnp.concatenate([k_norm_g, k_norm_g], axis=-1).reshape(depth, 1, LANES)
    norm_mix3, norm_ffn3 = vec3(norm_mix_g), vec3(norm_ffn_g)
    conv_b3, ln_g3, ln_b3 = vec3(conv_b), vec3(conv_ln_g), vec3(conv_ln_b)

    xt = x.reshape(T, D)
    for l in range(depth):
        q, kv, glu, gates = _in_projection(xt, mod4, norm_mix3, w_in_bf, l, S)
        attn_o = _attention(q, kv, cos128, sin128, gq128[l], gk128[l], attn_sinks[l], B, S)
        conv_o = _conv_branch(glu, conv_w, conv_b3, ln_g3, ln_b3, w_o_conv_bf, l, B, S)
        x_mid, h2 = _merge(xt, attn_o, conv_o, gates, w_o_attn_bf, w_out_bf, mod4, norm_ffn3, l, S)
        off, gate = _route(h2, w_query_bf, keys_bf, l)
        coef = _peer1(off, gate, h2.reshape(T, CHUNKS, LANES), u_bf, pick_sum, l)
        x3 = _peer2(off, coef, x_mid.reshape(T, CHUNKS, LANES), g2_all[l], v_bf, pick_spread, l, S)
        xt = x3.reshape(T, D)
    return xt.reshape(B, S, D)
```

```python
import numpy as np
import jax
import jax.numpy as jnp
from jax import lax
from jax.experimental import pallas as pl
from jax.experimental.pallas import tpu as pltpu

F32 = jnp.float32
BF16 = jnp.bfloat16

HEAD_DIM = 64
N_Q_HEADS = 16
N_KV_HEADS = 4
Q_PER_KV = N_Q_HEADS // N_KV_HEADS
WINDOW = 128
ROPE_THETA = 10000.0
CONV_WIDTH = 31
N_KEYS = 128
N_RET_HEADS = 8
TOPK = 16
N_PICKS = N_RET_HEADS * TOPK
EPS = 1e-6
NEG_INF = -1e30

LANES = 128
SUBLANES = 8
CHUNKS = 8
EXPERT_WORD_ROWS = CHUNKS // 2
MXU_TILE = 256
HALF_TILE = MXU_TILE // CHUNKS
PICKS_PER_TILE = 2 * HALF_TILE
N_TILES = N_PICKS // PICKS_PER_TILE
PICK_LANES = N_PICKS * CHUNKS


def _cparams(vmem_mb, sem):
    return pltpu.CompilerParams(dimension_semantics=sem, vmem_limit_bytes=vmem_mb * 1024 * 1024)


def _resident(block_shape, index_map):
    return pl.BlockSpec(block_shape, index_map, pipeline_mode=pl.Buffered(1))


def _split_bf16(v):
    hi = v.astype(BF16)
    lo = (v - hi.astype(F32)).astype(BF16)
    return hi, lo


def _dot(a, b):
    return jnp.dot(a, b, preferred_element_type=F32)


def _dot_nt(a, b):
    return lax.dot_general(a, b, (((1,), (1,)), ((), ())), preferred_element_type=F32)


def _rope_kernel(pos_ref, inv_ref, cs_ref):
    ang = inv_ref[...] * pos_ref[...].astype(F32)
    cs_ref[0:32, :] = jnp.cos(ang)
    cs_ref[32:64, :] = jnp.sin(ang)


def _rope_tables(positions):
    T = positions.size
    tb = min(2048, T)
    inv = ROPE_THETA ** (-jnp.arange(0, HEAD_DIM, 2, dtype=F32) / HEAD_DIM)
    cs = pl.pallas_call(
        _rope_kernel,
        grid=(T // tb,),
        in_specs=[pl.BlockSpec((1, tb), lambda i: (0, i)), pl.BlockSpec((32, 1), lambda i: (0, 0))],
        out_specs=pl.BlockSpec((64, tb), lambda i: (0, i)),
        out_shape=jax.ShapeDtypeStruct((64, T), F32),
        compiler_params=_cparams(32, ("arbitrary",)),
        name="rope_tables",
    )(positions.reshape(1, T), inv.reshape(32, 1))
    cos = cs[0:32].T
    sin = cs[32:64].T
    cos128 = jnp.concatenate([cos, cos, cos, cos], axis=1)
    sin128 = jnp.concatenate([-sin, sin, -sin, sin], axis=1)
    return cos128, sin128


def _mod_kernel(c_ref, w_ref, b_ref, o_ref):
    c = c_ref[...]
    ca = c * jax.nn.sigmoid(c)
    o_ref[...] = jnp.dot(ca, w_ref[...], preferred_element_type=F32,
                         precision=lax.Precision.HIGHEST) + b_ref[...]


def _modulation(c, ada_w, ada_b):
    depth, D, six_d = ada_w.shape
    B = c.shape[0]
    tn = 1024
    return pl.pallas_call(
        _mod_kernel,
        grid=(depth, six_d // tn),
        in_specs=[
            pl.BlockSpec((B, D), lambda l, j: (0, 0)),
            pl.BlockSpec((None, D, tn), lambda l, j: (l, 0, j)),
            pl.BlockSpec((None, 1, tn), lambda l, j: (l, 0, j)),
        ],
        out_specs=pl.BlockSpec((None, B, tn), lambda l, j: (l, 0, j)),
        out_shape=jax.ShapeDtypeStruct((depth, B, six_d), F32),
        compiler_params=_cparams(32, ("arbitrary", "arbitrary")),
        name="adaln_modulation",
    )(c, ada_w, ada_b.reshape(depth, 1, six_d))


def _in_kernel(x_ref, mod_ref, g_ref, w_ref, q_ref, kv_ref, glu_ref, gates_ref):
    D = x_ref.shape[1]
    x = x_ref[...]
    y = x * lax.rsqrt(jnp.mean(x * x, axis=-1, keepdims=True) + EPS) * g_ref[...]
    h = (y * (1.0 + mod_ref[:, D:2 * D]) + mod_ref[:, 0:D]).astype(BF16)
    c0 = 0
    for ref in (q_ref, kv_ref, glu_ref, gates_ref):
        n = ref.shape[1]
        ref[...] = _dot(h, w_ref[:, c0:c0 + n])
        c0 += n


def _in_projection(xt, mod4, norm_g, w_in_bf, l, S):
    T, D = xt.shape
    n_in = w_in_bf.shape[2]
    tm = 512
    q_w = N_Q_HEADS * HEAD_DIM
    kv_w = 2 * N_KV_HEADS * HEAD_DIM
    rest = (n_in - q_w - kv_w) // 2
    spb = S // tm
    widths = (q_w, kv_w, rest, rest)
    return pl.pallas_call(
        _in_kernel,
        grid=(T // tm,),
        in_specs=[
            pl.BlockSpec((tm, D), lambda i: (i, 0)),
            pl.BlockSpec((None, None, 1, mod4.shape[3]), lambda i: (l, i // spb, 0, 0)),
            pl.BlockSpec((None, 1, D), lambda i: (l, 0, 0)),
            _resident((None, D, n_in), lambda i: (l, 0, 0)),
        ],
        out_specs=[pl.BlockSpec((tm, w), lambda i: (i, 0)) for w in widths],
        out_shape=[jax.ShapeDtypeStruct((T, w), F32) for w in widths],
        compiler_params=_cparams(48, ("arbitrary",)),
        name="in_projection",
    )(xt, mod4, norm_g, w_in_bf)


def _norm_rope_chunk(raw, g, cos, sin, lane):
    sq = raw * raw
    low = lane < HEAD_DIM
    ss0 = jnp.sum(jnp.where(low, sq, 0.0), axis=-1, keepdims=True)
    ss1 = jnp.sum(jnp.where(low, 0.0, sq), axis=-1, keepdims=True)
    scale = jnp.where(low, lax.rsqrt(ss0 / HEAD_DIM + EPS), lax.rsqrt(ss1 / HEAD_DIM + EPS))
    xg = raw * scale * g
    first = (lane % HEAD_DIM) < (HEAD_DIM // 2)
    rot = jnp.where(first, pltpu.roll(xg, LANES - HEAD_DIM // 2, 1), pltpu.roll(xg, HEAD_DIM // 2, 1))
    return xg * cos + rot * sin


def _attn_kernel(sink_ref, q_ref, kvc_ref, kvp_ref, cosc_ref, sinc_ref, cosp_ref, sinp_ref,
                 gq_ref, gk_ref, o_ref, kd_scr, vd_scr, q_scr, s_scr, p_scr):
    tq = q_ref.shape[0]
    blk = WINDOW
    j = pl.program_id(1)
    kvw = N_KV_HEADS * HEAD_DIM
    gq = gq_ref[...]
    gk = gk_ref[...]

    for (src, cos_r, sin_r, r0, n) in ((kvp_ref, cosp_ref, sinp_ref, 0, blk),
                                       (kvc_ref, cosc_ref, sinc_ref, blk, tq)):
        lane = lax.broadcasted_iota(jnp.int32, (n, LANES), 1)
        low = lane < HEAD_DIM
        cos = cos_r[...]
        sin = sin_r[...]
        for c in range(kvw // LANES):
            kc = _norm_rope_chunk(src[:, c * LANES:(c + 1) * LANES], gk, cos, sin, lane)
            vc = src[:, kvw + c * LANES:kvw + (c + 1) * LANES]
            kc_sw = pltpu.roll(kc, HEAD_DIM, 1)
            vc_sw = pltpu.roll(vc, HEAD_DIM, 1)
            kd_scr[2 * c, r0:r0 + n, :] = jnp.where(low, kc, kc_sw).astype(BF16)
            kd_scr[2 * c + 1, r0:r0 + n, :] = jnp.where(low, kc_sw, kc).astype(BF16)
            vd_scr[2 * c, r0:r0 + n, :] = jnp.where(low, vc, vc_sw).astype(BF16)
            vd_scr[2 * c + 1, r0:r0 + n, :] = jnp.where(low, vc_sw, vc).astype(BF16)

    scale = HEAD_DIM ** -0.5
    lane_t = lax.broadcasted_iota(jnp.int32, (tq, LANES), 1)
    low_t = lane_t < HEAD_DIM
    for c in range(N_Q_HEADS // 2):
        qc = _norm_rope_chunk(q_ref[:, c * LANES:(c + 1) * LANES], gq, cosc_ref[...], sinc_ref[...],
                              lane_t) * scale
        q_scr[2 * c] = jnp.where(low_t, qc, 0.0).astype(BF16)
        q_scr[2 * c + 1] = jnp.where(low_t, 0.0, qc).astype(BF16)

    low_q = lax.broadcasted_iota(jnp.int32, (blk, LANES), 1) < HEAD_DIM
    qi = lax.broadcasted_iota(jnp.int32, (blk, 2 * blk), 0)
    kj = lax.broadcasted_iota(jnp.int32, (blk, 2 * blk), 1)
    diff = qi + blk - kj
    in_window = (diff >= 0) & (diff < WINDOW)

    for n in range(tq // blk):
        rows = slice(n * blk, (n + 1) * blk)
        keys = slice(n * blk, (n + 2) * blk)
        valid = in_window
        if n == 0:
            valid = valid & (kj >= jnp.where(j > 0, 0, blk))
        for g in range(N_KV_HEADS):
            q4 = jnp.concatenate([q_scr[g * Q_PER_KV + r, rows, :] for r in range(Q_PER_KV)], axis=0)
            s_scr[g] = _dot_nt(q4, kd_scr[g, keys, :])
        for g in range(N_KV_HEADS):
            for r in range(Q_PER_KV):
                hrows = slice(r * blk, (r + 1) * blk)
                s = jnp.where(valid, s_scr[g, hrows, :], NEG_INF)
                sink = sink_ref[g * Q_PER_KV + r]
                m = jnp.maximum(jnp.max(s, axis=-1, keepdims=True), sink)
                p = jnp.exp(s - m)
                p = p / (jnp.sum(p, axis=-1, keepdims=True) + jnp.exp(sink - m))
                p_scr[g, hrows, :] = p.astype(BF16)
        for g in range(N_KV_HEADS):
            o4 = _dot(p_scr[g], vd_scr[g, keys, :])
            for r2 in range(Q_PER_KV // 2):
                c = (g * Q_PER_KV) // 2 + r2
                oe = o4[(2 * r2) * blk:(2 * r2 + 1) * blk, :]
                oo = o4[(2 * r2 + 1) * blk:(2 * r2 + 2) * blk, :]
                o_ref[rows, c * LANES:(c + 1) * LANES] = jnp.where(low_q, oe, oo).astype(BF16)


def _attention(q, kv, cos128, sin128, gq, gk, sinks, B, S):
    T = q.shape[0]
    tq = 512
    blk = WINDOW
    nq = S // tq
    bpq = tq // blk
    cur = lambda b, j: (b * nq + j, 0)
    prev = lambda b, j: (jnp.maximum((b * nq + j) * bpq - 1, 0), 0)
    kvw = kv.shape[1]
    return pl.pallas_call(
        _attn_kernel,
        grid=(B, nq),
        in_specs=[
            pl.BlockSpec(memory_space=pltpu.SMEM),
            pl.BlockSpec((tq, q.shape[1]), cur),
            pl.BlockSpec((tq, kvw), cur),
            pl.BlockSpec((blk, kvw), prev),
            pl.BlockSpec((tq, LANES), cur),
            pl.BlockSpec((tq, LANES), cur),
            pl.BlockSpec((blk, LANES), prev),
            pl.BlockSpec((blk, LANES), prev),
            pl.BlockSpec((1, LANES), lambda b, j: (0, 0)),
            pl.BlockSpec((1, LANES), lambda b, j: (0, 0)),
        ],
        out_specs=pl.BlockSpec((tq, q.shape[1]), cur),
        scratch_shapes=[pltpu.VMEM((N_KV_HEADS, tq + blk, LANES), BF16),
                        pltpu.VMEM((N_KV_HEADS, tq + blk, LANES), BF16),
                        pltpu.VMEM((N_Q_HEADS, tq, LANES), BF16),
                        pltpu.VMEM((N_KV_HEADS, Q_PER_KV * blk, 2 * blk), F32),
                        pltpu.VMEM((N_KV_HEADS, Q_PER_KV * blk, 2 * blk), BF16)],
        out_shape=jax.ShapeDtypeStruct((T, q.shape[1]), BF16),
        compiler_params=_cparams(48, ("arbitrary", "arbitrary")),
        name="swa_attention",
    )(sinks, q, kv, kv, cos128, sin128, cos128, sin128, gq, gk)


HALO = 32


def _conv_kernel(glu_ref, cw_ref, cb_ref, lg_ref, lb_ref, wo_ref, o_ref, ubuf):
    tc = glu_ref.shape[0]
    C = o_ref.shape[1]
    j = pl.program_id(1)

    @pl.when(j == 0)
    def _():
        ubuf[0:HALO, :] = jnp.zeros((HALO, C), F32)
        ubuf[HALO + tc:, :] = jnp.zeros((SUBLANES, C), F32)

    @pl.when(j > 0)
    def _():
        ubuf[0:HALO, :] = ubuf[tc:tc + HALO, :]

    ubuf[HALO:HALO + tc, :] = glu_ref[:, 0:C] * jax.nn.sigmoid(glu_ref[:, C:2 * C])
    off = HALO - (CONV_WIDTH - 1)
    acc = cb_ref[...]
    for rem in range(SUBLANES):
        part = None
        for k in range(CONV_WIDTH):
            if (off + k) % SUBLANES == rem:
                base = off + k - rem
                term = cw_ref[k:k + 1, :] * ubuf[base:base + tc + SUBLANES, :]
                part = term if part is None else part + term
        acc = acc + part[rem:rem + tc, :]
    mu = jnp.mean(acc, axis=-1, keepdims=True)
    d = acc - mu
    var = jnp.mean(d * d, axis=-1, keepdims=True)
    yn = d * lax.rsqrt(var + EPS) * lg_ref[...] + lb_ref[...]
    act = yn * jax.nn.sigmoid(yn)
    o_ref[...] = _dot(act.astype(BF16), wo_ref[...])


def _conv_branch(glu, conv_w, conv_b, ln_g, ln_b, w_o_conv_bf, l, B, S):
    T = glu.shape[0]
    C = glu.shape[1] // 2
    tc = 256
    nc = S // tc
    vec = lambda: pl.BlockSpec((None, 1, C), lambda b, j: (l, 0, 0))
    return pl.pallas_call(
        _conv_kernel,
        grid=(B, nc),
        in_specs=[
            pl.BlockSpec((tc, 2 * C), lambda b, j: (b * nc + j, 0)),
            pl.BlockSpec((None, CONV_WIDTH, C), lambda b, j: (l, 0, 0)),
            vec(), vec(), vec(),
            _resident((None, C, C), lambda b, j: (l, 0, 0)),
        ],
        out_specs=pl.BlockSpec((tc, C), lambda b, j: (b * nc + j, 0)),
        out_shape=jax.ShapeDtypeStruct((T, C), F32),
        scratch_shapes=[pltpu.VMEM((tc + HALO + SUBLANES, C), F32)],
        compiler_params=_cparams(48, ("arbitrary", "arbitrary")),
        name="conv_branch",
    )(glu, conv_w, conv_b, ln_g, ln_b, w_o_conv_bf)


def _merge_kernel(x_ref, ao_ref, co_ref, gates_ref, woa_ref, wout_ref, mod_ref, gf_ref,
                  xmid_ref, h2_ref):
    D = x_ref.shape[1]
    attn = _dot(ao_ref[...], woa_ref[...])
    merged = (jax.nn.sigmoid(gates_ref[:, 0:D]) * attn
              + jax.nn.sigmoid(gates_ref[:, D:2 * D]) * co_ref[...])
    out = _dot(merged.astype(BF16), wout_ref[...])
    xm = x_ref[...] + mod_ref[:, 2 * D:3 * D] * out
    xmid_ref[...] = xm
    y = xm * lax.rsqrt(jnp.mean(xm * xm, axis=-1, keepdims=True) + EPS) * gf_ref[...]
    h2_ref[...] = y * (1.0 + mod_ref[:, 4 * D:5 * D]) + mod_ref[:, 3 * D:4 * D]


def _merge(xt, attn_o, conv_o, gates, w_o_attn_bf, w_out_bf, mod4, norm_ffn_g, l, S):
    T, D = xt.shape
    tm = 512
    spb = S // tm
    row = lambda w: pl.BlockSpec((tm, w), lambda i: (i, 0))
    return pl.pallas_call(
        _merge_kernel,
        grid=(T // tm,),
        in_specs=[
            row(D), row(D), row(D), row(2 * D),
            _resident((None, D, D), lambda i: (l, 0, 0)),
            _resident((None, D, D), lambda i: (l, 0, 0)),
            pl.BlockSpec((None, None, 1, mod4.shape[3]), lambda i: (l, i // spb, 0, 0)),
            pl.BlockSpec((None, 1, D), lambda i: (l, 0, 0)),
        ],
        out_specs=[row(D), row(D)],
        out_shape=[jax.ShapeDtypeStruct((T, D), F32), jax.ShapeDtypeStruct((T, D), F32)],
        compiler_params=_cparams(48, ("arbitrary",)),
        name="merge_out_projection",
    )(xt, attn_o, conv_o, gates, w_o_attn_bf, w_out_bf, mod4, norm_ffn_g)


def _topk_rows(s, payload, k):
    R, n = s.shape
    rid = lax.broadcasted_iota(jnp.int32, (R, n), 0).astype(F32)
    vals, pays = [], []
    for _ in range(k):
        m = jnp.max(s, axis=0, keepdims=True)
        pos = jnp.min(jnp.where(s == m, rid, float(R)), axis=0, keepdims=True)
        sel = rid == pos
        if payload is None:
            pays.append(pos)
        else:
            pays.append(jnp.max(jnp.where(sel, payload, -1.0), axis=0, keepdims=True))
        vals.append(m)
        s = jnp.where(sel, -jnp.inf, s)
    return jnp.concatenate(vals, axis=0), jnp.concatenate(pays, axis=0)


_CANDS = [(a, b) for a in range(TOPK) for b in range(TOPK) if (a + 1) * (b + 1) <= TOPK]


def _cand_rows(v0, v1):
    n_pad = -len(_CANDS) % SUBLANES
    ra = jnp.concatenate([v0[a:a + 1, :] for a, _ in _CANDS] + [v0[0:1, :]] * n_pad, axis=0)
    rb = jnp.concatenate([v1[b:b + 1, :] for _, b in _CANDS] + [v1[0:1, :]] * n_pad, axis=0)
    row = lax.broadcasted_iota(jnp.int32, ra.shape, 0)
    return ra, rb, row < len(_CANDS)


def _route_kernel(h_ref, wq_ref, keys_ref, off_ref, gate_ref):
    q = _dot(h_ref[...].astype(BF16), wq_ref[...])
    ids_all, gates_all = [], []
    for h in range(N_RET_HEADS):
        half = []
        for p in range(2):
            hp = 2 * h + p
            qs = q[:, hp * N_KEYS:(hp + 1) * N_KEYS].astype(BF16)
            st = _dot_nt(keys_ref[hp], qs)
            half.append(_topk_rows(st, None, TOPK))
        (v0, i0), (v1, i1) = half
        sa, sb, real = _cand_rows(v0, v1)
        ia, ib, _ = _cand_rows(i0, i1)
        cand_s = jnp.where(real, sa + sb, -jnp.inf)
        cand_i = ia * float(N_KEYS) + ib
        top_s, top_i = _topk_rows(cand_s, cand_i, TOPK)
        e = jnp.exp(top_s - top_s[0:1, :])
        gates_all.append(e / jnp.sum(e, axis=0, keepdims=True))
        ids_all.append(top_i)
    ids = jnp.concatenate(ids_all, axis=0).T.astype(jnp.int32)
    gate_ref[...] = jnp.concatenate(gates_all, axis=0).T
    off_ref[...] = ids * EXPERT_WORD_ROWS


def _route(h2, w_query_bf, keys_bf, l):
    T, D = h2.shape
    tr = 256
    nq = w_query_bf.shape[2]
    out = lambda dt: jax.ShapeDtypeStruct((T, N_PICKS), dt)
    return pl.pallas_call(
        _route_kernel,
        grid=(T // tr,),
        in_specs=[
            pl.BlockSpec((tr, D), lambda i: (i, 0)),
            _resident((None, D, nq), lambda i: (l, 0, 0)),
            _resident((None, 2 * N_RET_HEADS, N_KEYS, N_KEYS), lambda i: (l, 0, 0, 0)),
        ],
        out_specs=[pl.BlockSpec((tr, N_PICKS), lambda i: (i, 0))] * 2,
        out_shape=[out(jnp.int32), out(F32)],
        compiler_params=_cparams(32, ("arbitrary",)),
        name="peer_route",
    )(h2, w_query_bf, keys_bf)


TOKENS_PER_ITER = 32


def _bf16_bits(v):
    b = pltpu.bitcast(v, jnp.uint32)
    return (b + jnp.uint32(0x7FFF) + ((b >> 16) & jnp.uint32(1))) >> 16


def _pack_kernel(x_ref, o_ref):
    n_rows = x_ref.shape[0]
    words_per_expert = CHUNKS // 2
    for j in range(words_per_expert):
        lo = _bf16_bits(x_ref[:, (2 * j) * LANES:(2 * j + 1) * LANES])
        hi = _bf16_bits(x_ref[:, (2 * j + 1) * LANES:(2 * j + 2) * LANES])
        o_ref[pl.ds(j, n_rows, stride=words_per_expert), :] = lo | (hi << 16)


def _pack_table(tab):
    depth, n_exp, d = tab.shape
    rows_per_step = 512
    out_rows = rows_per_step * d // (2 * LANES)
    return pl.pallas_call(
        _pack_kernel,
        grid=(depth, n_exp // rows_per_step),
        in_specs=[pl.BlockSpec((None, rows_per_step, d), lambda l, i: (l, i, 0))],
        out_specs=pl.BlockSpec((None, out_rows, LANES), lambda l, i: (l, i, 0)),
        out_shape=jax.ShapeDtypeStruct((depth, n_exp * d // (2 * LANES), LANES), jnp.uint32),
        compiler_params=_cparams(32, ("arbitrary", "arbitrary")),
        name="pack_expert_table",
    )(tab)


def _expert_rows(tab_ref, off_ref, t, first, count):
    words = jnp.concatenate(
        [tab_ref[pl.ds(pl.multiple_of(off_ref.at[:, pl.ds(first + i, 1)][t, 0], EXPERT_WORD_ROWS),
                       EXPERT_WORD_ROWS), :]
         for i in range(count)], axis=0)
    return pltpu.bitcast(words, BF16)


def _tile_weights(tab_ref, off_ref, t, tile):
    base = tile * PICKS_PER_TILE
    wa = _expert_rows(tab_ref, off_ref, t, base, HALF_TILE)
    wb = _expert_rows(tab_ref, off_ref, t, base + HALF_TILE, HALF_TILE)
    return jnp.concatenate([wa, wb], axis=1)


def _stage_offsets(off_hbm, off_smem, sem):
    tq = off_smem.shape[0] // 2
    i = pl.program_id(0)
    slot = lax.rem(i, 2)

    def copy(step, s):
        return pltpu.make_async_copy(off_hbm.at[pl.ds(step * tq, tq), :],
                                     off_smem.at[pl.ds(s * tq, tq), :], sem.at[s])

    @pl.when(i == 0)
    def _():
        copy(0, 0).start()

    @pl.when(i + 1 < pl.num_programs(0))
    def _():
        copy(i + 1, 1 - slot).start()

    copy(i, slot).wait()
    return slot * tq


def _token_loop(n_tokens, token):
    def body(i, carry):
        for r in range(TOKENS_PER_ITER):
            token(i * TOKENS_PER_ITER + r)
        return carry

    lax.fori_loop(0, n_tokens // TOKENS_PER_ITER, body, 0)


def _peer1_kernel(off_hbm, gate_ref, h_ref, tab_ref, g_ref, coef_ref, u_scr, off_ref, off_sem):
    tq = h_ref.shape[0]
    row0 = _stage_offsets(off_hbm, off_ref, off_sem)
    half_w = MXU_TILE
    tile_w = 2 * half_w
    sub = lax.broadcasted_iota(jnp.int32, (CHUNKS, half_w), 0)
    lane = lax.broadcasted_iota(jnp.int32, (CHUNKS, half_w), 1)
    diag = (lane % CHUNKS) == sub
    zero = jnp.zeros((CHUNKS, LANES), BF16)

    def token(t):
        xhi, xlo = _split_bf16(h_ref[t])
        lhs = jnp.concatenate([
            jnp.concatenate([xhi, zero], axis=1), jnp.concatenate([xlo, zero], axis=1),
            jnp.concatenate([zero, xhi], axis=1), jnp.concatenate([zero, xlo], axis=1)], axis=0)
        for tile in range(N_TILES):
            s = _dot_nt(lhs, _tile_weights(tab_ref, off_ref, row0 + t, tile))
            sa = s[0:CHUNKS] + s[CHUNKS:2 * CHUNKS]
            sb = s[2 * CHUNKS:3 * CHUNKS] + s[3 * CHUNKS:4 * CHUNKS]
            ua = jnp.sum(jnp.where(diag, sa, 0.0), axis=0, keepdims=True)
            ub = jnp.sum(jnp.where(diag, sb, 0.0), axis=0, keepdims=True)
            u_scr[pl.ds(t, 1), tile * tile_w:tile * tile_w + half_w] = ua
            u_scr[pl.ds(t, 1), tile * tile_w + half_w:(tile + 1) * tile_w] = ub

    _token_loop(tq, token)
    uhi, ulo = _split_bf16(u_scr[...])
    a = _dot(uhi, g_ref[...]) + _dot(ulo, g_ref[...])
    gelu = 0.5 * a * (1.0 + lax.erf(a * (2.0 ** -0.5)))
    coef_ref[...] = gate_ref[...] * gelu


def _peer2_kernel(off_hbm, coef_ref, x_ref, g2_ref, tab_ref, e_ref, o_ref, c_scr, off_ref, off_sem):
    tq = x_ref.shape[0]
    row0 = _stage_offsets(off_hbm, off_ref, off_sem)
    half_w = MXU_TILE
    tile_w = 2 * half_w
    chi, clo = _split_bf16(coef_ref[...])
    c_scr[...] = _dot(chi, e_ref[...]) + _dot(clo, e_ref[...])
    sub = lax.broadcasted_iota(jnp.int32, (2 * CHUNKS, half_w), 0)
    lane = lax.broadcasted_iota(jnp.int32, (2 * CHUNKS, half_w), 1)
    diag = (lane % CHUNKS) == (sub % CHUNKS)
    group_a = sub < CHUNKS
    g2 = g2_ref[...]

    def token(t):
        acc = jnp.zeros((4 * CHUNKS, 2 * LANES), F32)
        for tile in range(N_TILES):
            ra = c_scr[pl.ds(t, 1), tile * tile_w:tile * tile_w + half_w]
            rb = c_scr[pl.ds(t, 1), tile * tile_w + half_w:(tile + 1) * tile_w]
            both = jnp.where(group_a, jnp.broadcast_to(ra, diag.shape), jnp.broadcast_to(rb, diag.shape))
            lhi, llo = _split_bf16(jnp.where(diag, both, 0.0))
            lhs = jnp.concatenate([lhi, llo], axis=0)
            acc = acc + _dot(lhs, _tile_weights(tab_ref, off_ref, row0 + t, tile))
        y = (acc[0:CHUNKS, 0:LANES] + acc[2 * CHUNKS:3 * CHUNKS, 0:LANES]
             + acc[CHUNKS:2 * CHUNKS, LANES:] + acc[3 * CHUNKS:, LANES:])
        o_ref[t] = x_ref[t] + g2 * y

    _token_loop(tq, token)


def _peer_constants():
    sel = (np.arange(PICK_LANES)[:, None] // CHUNKS) == np.arange(N_PICKS)[None, :]
    g = jnp.asarray(sel.astype(np.float32), dtype=BF16)
    return g, g.T


def _peer_tokens_per_step():
    return 256


def _peer1(off, gate, h3, tab, g, l):
    T = off.shape[0]
    tq = _peer_tokens_per_step()
    row = lambda: pl.BlockSpec((tq, N_PICKS), lambda i: (i, 0))
    return pl.pallas_call(
        _peer1_kernel,
        grid=(T // tq,),
        in_specs=[
            pl.BlockSpec(memory_space=pl.ANY),
            row(),
            pl.BlockSpec((tq, CHUNKS, LANES), lambda i: (i, 0, 0)),
            _resident((None,) + tab.shape[1:], lambda i: (l, 0, 0)),
            _resident((PICK_LANES, N_PICKS), lambda i: (0, 0)),
        ],
        out_specs=row(),
        out_shape=jax.ShapeDtypeStruct((T, N_PICKS), F32),
        scratch_shapes=[pltpu.VMEM((tq, PICK_LANES), F32),
                        pltpu.SMEM((2 * tq, N_PICKS), jnp.int32),
                        pltpu.SemaphoreType.DMA((2,))],
        compiler_params=_cparams(48, ("arbitrary",)),
        name="peer_scores",
    )(off, gate, h3, tab, g)


def _peer2(off, coef, x3, g2, tab, e, l, S):
    T = off.shape[0]
    tq = _peer_tokens_per_step()
    spb = S // tq
    row = lambda: pl.BlockSpec((tq, N_PICKS), lambda i: (i, 0))
    tok = lambda: pl.BlockSpec((tq, CHUNKS, LANES), lambda i: (i, 0, 0))
    return pl.pallas_call(
        _peer2_kernel,
        grid=(T // tq,),
        in_specs=[
            pl.BlockSpec(memory_space=pl.ANY),
            row(), tok(),
            pl.BlockSpec((None, CHUNKS, LANES), lambda i: (i // spb, 0, 0)),
            _resident((None,) + tab.shape[1:], lambda i: (l, 0, 0)),
            _resident((N_PICKS, PICK_LANES), lambda i: (0, 0)),
        ],
        out_specs=tok(),
        out_shape=jax.ShapeDtypeStruct((T, CHUNKS, LANES), F32),
        scratch_shapes=[pltpu.VMEM((tq, PICK_LANES), F32),
                        pltpu.SMEM((2 * tq, N_PICKS), jnp.int32),
                        pltpu.SemaphoreType.DMA((2,))],
        compiler_params=_cparams(48, ("arbitrary",)),
        name="peer_combine",
    )(off, coef, x3, g2, tab, e)


def kernel(x, c, positions, ada_w, ada_b, norm_mix_g, w_in, q_norm_g, k_norm_g, attn_sinks,
           w_o_attn, conv_w, conv_b, conv_ln_g, conv_ln_b, w_o_conv, w_out, norm_ffn_g,
           peer_w_query, peer_sub_keys, peer_u, peer_v):
    B, S, D = x.shape
    depth = ada_w.shape[0]
    T = B * S
    assert D == CHUNKS * LANES and S % 512 == 0

    cos128, sin128 = _rope_tables(positions)
    mod = _modulation(c, ada_w, ada_b)
    mod4 = mod.reshape(depth, B, 1, 6 * D)
    g2_all = mod[:, :, 5 * D:6 * D].reshape(depth, B, CHUNKS, LANES)

    w_in_bf = w_in.astype(BF16)
    w_o_attn_bf = w_o_attn.astype(BF16)
    w_o_conv_bf = w_o_conv.astype(BF16)
    w_out_bf = w_out.astype(BF16)
    w_query_bf = peer_w_query.astype(BF16)
    keys_bf = peer_sub_keys.astype(BF16).reshape(depth, 2 * N_RET_HEADS, N_KEYS, peer_sub_keys.shape[-1])
    u_bf = _pack_table(peer_u)
    v_bf = _pack_table(peer_v)
    pick_sum, pick_spread = _peer_constants()
    vec3 = lambda a: a.reshape(depth, 1, a.shape[-1])
    gq128 = jnp.concatenate([q_norm_g, q_norm_g], axis=-1).reshape(depth, 1, LANES)
    gk128 = jnp.concatenate([k_norm_g, k_norm_g], axis=-1).reshape(depth, 1, LANES)
    norm_mix3, norm_ffn3 = vec3(norm_mix_g), vec3(norm_ffn_g)
    conv_b3, ln_g3, ln_b3 = vec3(conv_b), vec3(conv_ln_g), vec3(conv_ln_b)

    xt = x.reshape(T, D)
    for l in range(depth):
        q, kv, glu, gates = _in_projection(xt, mod4, norm_mix3, w_in_bf, l, S)
        attn_o = _attention(q, kv, cos128, sin128, gq128[l], gk128[l], attn_sinks[l], B, S)
        conv_o = _conv_branch(glu, conv_w, conv_b3, ln_g3, ln_b3, w_o_conv_bf, l, B, S)
        x_mid, h2 = _merge(xt, attn_o, conv_o, gates, w_o_attn_bf, w_out_bf, mod4, norm_ffn3, l, S)
        off, gate = _route(h2, w_query_bf, keys_bf, l)
        coef = _peer1(off, gate, h2.reshape(T, CHUNKS, LANES), u_bf, pick_sum, l)
        x3 = _peer2(off, coef, x_mid.reshape(T, CHUNKS, LANES), g2_all[l], v_bf, pick_spread, l, S)
        xt = x3.reshape(T, D)
    return xt.reshape(B, S, D)
```

```python
import numpy as np
import jax
import jax.numpy as jnp
from jax import lax
from jax.experimental import pallas as pl
from jax.experimental.pallas import tpu as pltpu

F32 = jnp.float32
BF16 = jnp.bfloat16

HEAD_DIM = 64
N_Q_HEADS = 16
N_KV_HEADS = 4
Q_PER_KV = N_Q_HEADS // N_KV_HEADS
WINDOW = 128
ROPE_THETA = 10000.0
CONV_WIDTH = 31
N_KEYS = 128
N_RET_HEADS = 8
TOPK = 16
N_PICKS = N_RET_HEADS * TOPK
EPS = 1e-6
NEG_INF = -1e30

LANES = 128
SUBLANES = 8
CHUNKS = 8
EXPERT_WORD_ROWS = CHUNKS // 2
MXU_TILE = 256
HALF_TILE = MXU_TILE // CHUNKS
PICKS_PER_TILE = 2 * HALF_TILE
N_TILES = N_PICKS // PICKS_PER_TILE
PICK_LANES = N_PICKS * CHUNKS


def _cparams(vmem_mb, sem):
    return pltpu.CompilerParams(dimension_semantics=sem, vmem_limit_bytes=vmem_mb * 1024 * 1024)


def _resident(block_shape, index_map):
    return pl.BlockSpec(block_shape, index_map, pipeline_mode=pl.Buffered(1))


def _split_bf16(v):
    hi = v.astype(BF16)
    lo = (v - hi.astype(F32)).astype(BF16)
    return hi, lo


def _dot(a, b):
    return jnp.dot(a, b, preferred_element_type=F32)


def _dot_nt(a, b):
    return lax.dot_general(a, b, (((1,), (1,)), ((), ())), preferred_element_type=F32)


def _rope_kernel(pos_ref, inv_ref, cs_ref):
    ang = inv_ref[...] * pos_ref[...].astype(F32)
    cs_ref[0:32, :] = jnp.cos(ang)
    cs_ref[32:64, :] = jnp.sin(ang)


def _rope_tables(positions):
    T = positions.size
    tb = min(2048, T)
    inv = ROPE_THETA ** (-jnp.arange(0, HEAD_DIM, 2, dtype=F32) / HEAD_DIM)
    cs = pl.pallas_call(
        _rope_kernel,
        grid=(T // tb,),
        in_specs=[pl.BlockSpec((1, tb), lambda i: (0, i)), pl.BlockSpec((32, 1), lambda i: (0, 0))],
        out_specs=pl.BlockSpec((64, tb), lambda i: (0, i)),
        out_shape=jax.ShapeDtypeStruct((64, T), F32),
        compiler_params=_cparams(32, ("arbitrary",)),
        name="rope_tables",
    )(positions.reshape(1, T), inv.reshape(32, 1))
    cos = cs[0:32].T
    sin = cs[32:64].T
    cos128 = jnp.concatenate([cos, cos, cos, cos], axis=1)
    sin128 = jnp.concatenate([-sin, sin, -sin, sin], axis=1)
    return cos128, sin128


def _mod_kernel(c_ref, w_ref, b_ref, o_ref):
    c = c_ref[...]
    ca = c * jax.nn.sigmoid(c)
    o_ref[...] = jnp.dot(ca, w_ref[...], preferred_element_type=F32,
                         precision=lax.Precision.HIGHEST) + b_ref[...]


def _modulation(c, ada_w, ada_b):
    depth, D, six_d = ada_w.shape
    B = c.shape[0]
    tn = 1024
    return pl.pallas_call(
        _mod_kernel,
        grid=(depth, six_d // tn),
        in_specs=[
            pl.BlockSpec((B, D), lambda l, j: (0, 0)),
            pl.BlockSpec((None, D, tn), lambda l, j: (l, 0, j)),
            pl.BlockSpec((None, 1, tn), lambda l, j: (l, 0, j)),
        ],
        out_specs=pl.BlockSpec((None, B, tn), lambda l, j: (l, 0, j)),
        out_shape=jax.ShapeDtypeStruct((depth, B, six_d), F32),
        compiler_params=_cparams(32, ("arbitrary", "arbitrary")),
        name="adaln_modulation",
    )(c, ada_w, ada_b.reshape(depth, 1, six_d))


def _in_kernel(x_ref, mod_ref, g_ref, w_ref, q_ref, kv_ref, glu_ref, gates_ref):
    D = x_ref.shape[1]
    x = x_ref[...]
    y = x * lax.rsqrt(jnp.mean(x * x, axis=-1, keepdims=True) + EPS) * g_ref[...]
    h = (y * (1.0 + mod_ref[:, D:2 * D]) + mod_ref[:, 0:D]).astype(BF16)
    c0 = 0
    for ref in (q_ref, kv_ref, glu_ref, gates_ref):
        n = ref.shape[1]
        ref[...] = _dot(h, w_ref[:, c0:c0 + n])
        c0 += n


def _in_projection(xt, mod4, norm_g, w_in_bf, l, S):
    T, D = xt.shape
    n_in = w_in_bf.shape[2]
    tm = 512
    q_w = N_Q_HEADS * HEAD_DIM
    kv_w = 2 * N_KV_HEADS * HEAD_DIM
    rest = (n_in - q_w - kv_w) // 2
    spb = S // tm
    widths = (q_w, kv_w, rest, rest)
    return pl.pallas_call(
        _in_kernel,
        grid=(T // tm,),
        in_specs=[
            pl.BlockSpec((tm, D), lambda i: (i, 0)),
            pl.BlockSpec((None, None, 1, mod4.shape[3]), lambda i: (l, i // spb, 0, 0)),
            pl.BlockSpec((None, 1, D), lambda i: (l, 0, 0)),
            _resident((None, D, n_in), lambda i: (l, 0, 0)),
        ],
        out_specs=[pl.BlockSpec((tm, w), lambda i: (i, 0)) for w in widths],
        out_shape=[jax.ShapeDtypeStruct((T, w), F32) for w in widths],
        compiler_params=_cparams(48, ("arbitrary",)),
        name="in_projection",
    )(xt, mod4, norm_g, w_in_bf)


def _norm_rope_chunk(raw, g, cos, sin, lane):
    sq = raw * raw
    low = lane < HEAD_DIM
    ss0 = jnp.sum(jnp.where(low, sq, 0.0), axis=-1, keepdims=True)
    ss1 = jnp.sum(jnp.where(low, 0.0, sq), axis=-1, keepdims=True)
    scale = jnp.where(low, lax.rsqrt(ss0 / HEAD_DIM + EPS), lax.rsqrt(ss1 / HEAD_DIM + EPS))
    xg = raw * scale * g
    first = (lane % HEAD_DIM) < (HEAD_DIM // 2)
    rot = jnp.where(first, pltpu.roll(xg, LANES - HEAD_DIM // 2, 1), pltpu.roll(xg, HEAD_DIM // 2, 1))
    return xg * cos + rot * sin


def _attn_kernel(sink_ref, q_ref, kvc_ref, kvp_ref, cosc_ref, sinc_ref, cosp_ref, sinp_ref,
                 gq_ref, gk_ref, o_ref, kd_scr, vd_scr, q_scr, s_scr, p_scr):
    tq = q_ref.shape[0]
    blk = WINDOW
    j = pl.program_id(1)
    kvw = N_KV_HEADS * HEAD_DIM
    gq = gq_ref[...]
    gk = gk_ref[...]

    for (src, cos_r, sin_r, r0, n) in ((kvp_ref, cosp_ref, sinp_ref, 0, blk),
                                       (kvc_ref, cosc_ref, sinc_ref, blk, tq)):
        lane = lax.broadcasted_iota(jnp.int32, (n, LANES), 1)
        low = lane < HEAD_DIM
        cos = cos_r[...]
        sin = sin_r[...]
        for c in range(kvw // LANES):
            kc = _norm_rope_chunk(src[:, c * LANES:(c + 1) * LANES], gk, cos, sin, lane)
            vc = src[:, kvw + c * LANES:kvw + (c + 1) * LANES]
            kc_sw = pltpu.roll(kc, HEAD_DIM, 1)
            vc_sw = pltpu.roll(vc, HEAD_DIM, 1)
            kd_scr[2 * c, r0:r0 + n, :] = jnp.where(low, kc, kc_sw).astype(BF16)
            kd_scr[2 * c + 1, r0:r0 + n, :] = jnp.where(low, kc_sw, kc).astype(BF16)
            vd_scr[2 * c, r0:r0 + n, :] = jnp.where(low, vc, vc_sw).astype(BF16)
            vd_scr[2 * c + 1, r0:r0 + n, :] = jnp.where(low, vc_sw, vc).astype(BF16)

    scale = HEAD_DIM ** -0.5
    lane_t = lax.broadcasted_iota(jnp.int32, (tq, LANES), 1)
    low_t = lane_t < HEAD_DIM
    for c in range(N_Q_HEADS // 2):
        qc = _norm_rope_chunk(q_ref[:, c * LANES:(c + 1) * LANES], gq, cosc_ref[...], sinc_ref[...],
                              lane_t) * scale
        q_scr[2 * c] = jnp.where(low_t, qc, 0.0).astype(BF16)
        q_scr[2 * c + 1] = jnp.where(low_t, 0.0, qc).astype(BF16)

    low_q = lax.broadcasted_iota(jnp.int32, (blk, LANES), 1) < HEAD_DIM
    qi = lax.broadcasted_iota(jnp.int32, (blk, 2 * blk), 0)
    kj = lax.broadcasted_iota(jnp.int32, (blk, 2 * blk), 1)
    diff = qi + blk - kj
    in_window = (diff >= 0) & (diff < WINDOW)

    for n in range(tq // blk):
        rows = slice(n * blk, (n + 1) * blk)
        keys = slice(n * blk, (n + 2) * blk)
        valid = in_window
        if n == 0:
            valid = valid & (kj >= jnp.where(j > 0, 0, blk))
        for g in range(N_KV_HEADS):
            q4 = jnp.concatenate([q_scr[g * Q_PER_KV + r, rows, :] for r in range(Q_PER_KV)], axis=0)
            s_scr[g] = _dot_nt(q4, kd_scr[g, keys, :])
        for g in range(N_KV_HEADS):
            for r in range(Q_PER_KV):
                hrows = slice(r * blk, (r + 1) * blk)
                s = jnp.where(valid, s_scr[g, hrows, :], NEG_INF)
                sink = sink_ref[g * Q_PER_KV + r]
                m = jnp.maximum(jnp.max(s, axis=-1, keepdims=True), sink)
                p = jnp.exp(s - m)
                p = p / (jnp.sum(p, axis=-1, keepdims=True) + jnp.exp(sink - m))
                p_scr[g, hrows, :] = p.astype(BF16)
        for g in range(N_KV_HEADS):
            o4 = _dot(p_scr[g], vd_scr[g, keys, :])
            for r2 in range(Q_PER_KV // 2):
                c = (g * Q_PER_KV) // 2 + r2
                oe = o4[(2 * r2) * blk:(2 * r2 + 1) * blk, :]
                oo = o4[(2 * r2 + 1) * blk:(2 * r2 + 2) * blk, :]
                o_ref[rows, c * LANES:(c + 1) * LANES] = jnp.where(low_q, oe, oo).astype(BF16)


def _attention(q, kv, cos128, sin128, gq, gk, sinks, B, S):
    T = q.shape[0]
    tq = 512
    blk = WINDOW
    nq = S // tq
    bpq = tq // blk
    cur = lambda b, j: (b * nq + j, 0)
    prev = lambda b, j: (jnp.maximum((b * nq + j) * bpq - 1, 0), 0)
    kvw = kv.shape[1]
    return pl.pallas_call(
        _attn_kernel,
        grid=(B, nq),
        in_specs=[
            pl.BlockSpec(memory_space=pltpu.SMEM),
            pl.BlockSpec((tq, q.shape[1]), cur),
            pl.BlockSpec((tq, kvw), cur),
            pl.BlockSpec((blk, kvw), prev),
            pl.BlockSpec((tq, LANES), cur),
            pl.BlockSpec((tq, LANES), cur),
            pl.BlockSpec((blk, LANES), prev),
            pl.BlockSpec((blk, LANES), prev),
            pl.BlockSpec((1, LANES), lambda b, j: (0, 0)),
            pl.BlockSpec((1, LANES), lambda b, j: (0, 0)),
        ],
        out_specs=pl.BlockSpec((tq, q.shape[1]), cur),
        scratch_shapes=[pltpu.VMEM((N_KV_HEADS, tq + blk, LANES), BF16),
                        pltpu.VMEM((N_KV_HEADS, tq + blk, LANES), BF16),
                        pltpu.VMEM((N_Q_HEADS, tq, LANES), BF16),
                        pltpu.VMEM((N_KV_HEADS, Q_PER_KV * blk, 2 * blk), F32),
                        pltpu.VMEM((N_KV_HEADS, Q_PER_KV * blk, 2 * blk), BF16)],
        out_shape=jax.ShapeDtypeStruct((T, q.shape[1]), BF16),
        compiler_params=_cparams(48, ("arbitrary", "arbitrary")),
        name="swa_attention",
    )(sinks, q, kv, kv, cos128, sin128, cos128, sin128, gq, gk)


HALO = 32


def _conv_kernel(glu_ref, cw_ref, cb_ref, lg_ref, lb_ref, wo_ref, o_ref, ubuf):
    tc = glu_ref.shape[0]
    C = o_ref.shape[1]
    j = pl.program_id(1)

    @pl.when(j == 0)
    def _():
        ubuf[0:HALO, :] = jnp.zeros((HALO, C), F32)
        ubuf[HALO + tc:, :] = jnp.zeros((SUBLANES, C), F32)

    @pl.when(j > 0)
    def _():
        ubuf[0:HALO, :] = ubuf[tc:tc + HALO, :]

    ubuf[HALO:HALO + tc, :] = glu_ref[:, 0:C] * jax.nn.sigmoid(glu_ref[:, C:2 * C])
    off = HALO - (CONV_WIDTH - 1)
    acc = cb_ref[...]
    for rem in range(SUBLANES):
        part = None
        for k in range(CONV_WIDTH):
            if (off + k) % SUBLANES == rem:
                base = off + k - rem
                term = cw_ref[k:k + 1, :] * ubuf[base:base + tc + SUBLANES, :]
                part = term if part is None else part + term
        acc = acc + part[rem:rem + tc, :]
    mu = jnp.mean(acc, axis=-1, keepdims=True)
    d = acc - mu
    var = jnp.mean(d * d, axis=-1, keepdims=True)
    yn = d * lax.rsqrt(var + EPS) * lg_ref[...] + lb_ref[...]
    act = yn * jax.nn.sigmoid(yn)
    o_ref[...] = _dot(act.astype(BF16), wo_ref[...])


def _conv_branch(glu, conv_w, conv_b, ln_g, ln_b, w_o_conv_bf, l, B, S):
    T = glu.shape[0]
    C = glu.shape[1] // 2
    tc = 256
    nc = S // tc
    vec = lambda: pl.BlockSpec((None, 1, C), lambda b, j: (l, 0, 0))
    return pl.pallas_call(
        _conv_kernel,
        grid=(B, nc),
        in_specs=[
            pl.BlockSpec((tc, 2 * C), lambda b, j: (b * nc + j, 0)),
            pl.BlockSpec((None, CONV_WIDTH, C), lambda b, j: (l, 0, 0)),
            vec(), vec(), vec(),
            _resident((None, C, C), lambda b, j: (l, 0, 0)),
        ],
        out_specs=pl.BlockSpec((tc, C), lambda b, j: (b * nc + j, 0)),
        out_shape=jax.ShapeDtypeStruct((T, C), F32),
        scratch_shapes=[pltpu.VMEM((tc + HALO + SUBLANES, C), F32)],
        compiler_params=_cparams(48, ("arbitrary", "arbitrary")),
        name="conv_branch",
    )(glu, conv_w, conv_b, ln_g, ln_b, w_o_conv_bf)


def _merge_route_kernel(x_ref, ao_ref, co_ref, gates_ref, woa_ref, wout_ref, mod_ref, gf_ref,
                        wq_ref, keys_ref, xmid_ref, h2_ref, off_ref, gate_ref):
    D = x_ref.shape[1]
    attn = _dot(ao_ref[...], woa_ref[...])
    merged = (jax.nn.sigmoid(gates_ref[:, 0:D]) * attn
              + jax.nn.sigmoid(gates_ref[:, D:2 * D]) * co_ref[...])
    out = _dot(merged.astype(BF16), wout_ref[...])
    xm = x_ref[...] + mod_ref[:, 2 * D:3 * D] * out
    xmid_ref[...] = xm
    y = xm * lax.rsqrt(jnp.mean(xm * xm, axis=-1, keepdims=True) + EPS) * gf_ref[...]
    h2 = y * (1.0 + mod_ref[:, 4 * D:5 * D]) + mod_ref[:, 3 * D:4 * D]
    h2_ref[...] = h2
    _route_body(h2, wq_ref, keys_ref, off_ref, gate_ref)


def _merge_route(xt, attn_o, conv_o, gates, w_o_attn_bf, w_out_bf, mod4, norm_ffn_g, w_query_bf, keys_bf, l, S):
    T, D = xt.shape
    tm = 256
    spb = S // tm
    nq = w_query_bf.shape[2]
    row = lambda w: pl.BlockSpec((tm, w), lambda i: (i, 0))
    return pl.pallas_call(
        _merge_route_kernel,
        grid=(T // tm,),
        in_specs=[
            row(D), row(D), row(D), row(2 * D),
            _resident((None, D, D), lambda i: (l, 0, 0)),
            _resident((None, D, D), lambda i: (l, 0, 0)),
            pl.BlockSpec((None, None, 1, mod4.shape[3]), lambda i: (l, i // spb, 0, 0)),
            pl.BlockSpec((None, 1, D), lambda i: (l, 0, 0)),
            _resident((None, D, nq), lambda i: (l, 0, 0)),
            _resident((None, 2 * N_RET_HEADS, N_KEYS, N_KEYS), lambda i: (l, 0, 0, 0)),
        ],
        out_specs=[row(D), row(D), row(N_PICKS), row(N_PICKS)],
        out_shape=[jax.ShapeDtypeStruct((T, D), F32), jax.ShapeDtypeStruct((T, D), F32),
                   jax.ShapeDtypeStruct((T, N_PICKS), jnp.int32), jax.ShapeDtypeStruct((T, N_PICKS), F32)],
        compiler_params=_cparams(48, ("arbitrary",)),
        name="merge_projection_and_route",
    )(xt, attn_o, conv_o, gates, w_o_attn_bf, w_out_bf, mod4, norm_ffn_g, w_query_bf, keys_bf)


def _topk_rows(s, payload, k):
    R, n = s.shape
    rid = lax.broadcasted_iota(jnp.int32, (R, n), 0).astype(F32)
    vals, pays = [], []
    for _ in range(k):
        m = jnp.max(s, axis=0, keepdims=True)
        pos = jnp.min(jnp.where(s == m, rid, float(R)), axis=0, keepdims=True)
        sel = rid == pos
        if payload is None:
            pays.append(pos)
        else:
            pays.append(jnp.max(jnp.where(sel, payload, -1.0), axis=0, keepdims=True))
        vals.append(m)
        s = jnp.where(sel, -jnp.inf, s)
    return jnp.concatenate(vals, axis=0), jnp.concatenate(pays, axis=0)


_CANDS = [(a, b) for a in range(TOPK) for b in range(TOPK) if (a + 1) * (b + 1) <= TOPK]


def _cand_rows(v0, v1):
    n_pad = -len(_CANDS) % SUBLANES
    ra = jnp.concatenate([v0[a:a + 1, :] for a, _ in _CANDS] + [v0[0:1, :]] * n_pad, axis=0)
    rb = jnp.concatenate([v1[b:b + 1, :] for _, b in _CANDS] + [v1[0:1, :]] * n_pad, axis=0)
    row = lax.broadcasted_iota(jnp.int32, ra.shape, 0)
    return ra, rb, row < len(_CANDS)


def _route_body(h, wq_ref, keys_ref, off_ref, gate_ref):
    q = _dot(h.astype(BF16), wq_ref[...])
    ids_all, gates_all = [], []
    for h in range(N_RET_HEADS):
        half = []
        for p in range(2):
            hp = 2 * h + p
            qs = q[:, hp * N_KEYS:(hp + 1) * N_KEYS].astype(BF16)
            st = _dot_nt(keys_ref[hp], qs)
            half.append(_topk_rows(st, None, TOPK))
        (v0, i0), (v1, i1) = half
        sa, sb, real = _cand_rows(v0, v1)
        ia, ib, _ = _cand_rows(i0, i1)
        cand_s = jnp.where(real, sa + sb, -jnp.inf)
        cand_i = ia * float(N_KEYS) + ib
        top_s, top_i = _topk_rows(cand_s, cand_i, TOPK)
        e = jnp.exp(top_s - top_s[0:1, :])
        gates_all.append(e / jnp.sum(e, axis=0, keepdims=True))
        ids_all.append(top_i)
    ids = jnp.concatenate(ids_all, axis=0).T.astype(jnp.int32)
    gate_ref[...] = jnp.concatenate(gates_all, axis=0).T
    off_ref[...] = ids * EXPERT_WORD_ROWS


TOKENS_PER_ITER = 32


def _bf16_bits(v):
    b = pltpu.bitcast(v, jnp.uint32)
    return (b + jnp.uint32(0x7FFF) + ((b >> 16) & jnp.uint32(1))) >> 16


def _pack_kernel(x_ref, o_ref):
    n_rows = x_ref.shape[0]
    words_per_expert = CHUNKS // 2
    for j in range(words_per_expert):
        lo = _bf16_bits(x_ref[:, (2 * j) * LANES:(2 * j + 1) * LANES])
        hi = _bf16_bits(x_ref[:, (2 * j + 1) * LANES:(2 * j + 2) * LANES])
        o_ref[pl.ds(j, n_rows, stride=words_per_expert), :] = lo | (hi << 16)


def _pack_table(tab):
    depth, n_exp, d = tab.shape
    rows_per_step = 512
    out_rows = rows_per_step * d // (2 * LANES)
    return pl.pallas_call(
        _pack_kernel,
        grid=(depth, n_exp // rows_per_step),
        in_specs=[pl.BlockSpec((None, rows_per_step, d), lambda l, i: (l, i, 0))],
        out_specs=pl.BlockSpec((None, out_rows, LANES), lambda l, i: (l, i, 0)),
        out_shape=jax.ShapeDtypeStruct((depth, n_exp * d // (2 * LANES), LANES), jnp.uint32),
        compiler_params=_cparams(32, ("arbitrary", "arbitrary")),
        name="pack_expert_table",
    )(tab)


def _expert_rows(tab_ref, off_ref, t, first, count):
    words = jnp.concatenate(
        [tab_ref[pl.ds(pl.multiple_of(off_ref.at[:, pl.ds(first + i, 1)][t, 0], EXPERT_WORD_ROWS),
                       EXPERT_WORD_ROWS), :]
         for i in range(count)], axis=0)
    return pltpu.bitcast(words, BF16)


def _tile_weights(tab_ref, off_ref, t, tile):
    base = tile * PICKS_PER_TILE
    wa = _expert_rows(tab_ref, off_ref, t, base, HALF_TILE)
    wb = _expert_rows(tab_ref, off_ref, t, base + HALF_TILE, HALF_TILE)
    return jnp.concatenate([wa, wb], axis=1)


def _stage_offsets(off_hbm, off_smem, sem):
    tq = off_smem.shape[0] // 2
    i = pl.program_id(0)
    slot = lax.rem(i, 2)

    def copy(step, s):
        return pltpu.make_async_copy(off_hbm.at[pl.ds(step * tq, tq), :],
                                     off_smem.at[pl.ds(s * tq, tq), :], sem.at[s])

    @pl.when(i == 0)
    def _():
        copy(0, 0).start()

    @pl.when(i + 1 < pl.num_programs(0))
    def _():
        copy(i + 1, 1 - slot).start()

    copy(i, slot).wait()
    return slot * tq


def _token_loop(n_tokens, token):
    def body(i, carry):
        for r in range(TOKENS_PER_ITER):
            token(i * TOKENS_PER_ITER + r)
        return carry

    lax.fori_loop(0, n_tokens // TOKENS_PER_ITER, body, 0)


def _peer1_kernel(off_hbm, gate_ref, h_ref, tab_ref, g_ref, coef_ref, u_scr, off_ref, off_sem):
    tq = h_ref.shape[0]
    row0 = _stage_offsets(off_hbm, off_ref, off_sem)
    half_w = MXU_TILE
    tile_w = 2 * half_w
    sub = lax.broadcasted_iota(jnp.int32, (CHUNKS, half_w), 0)
    lane = lax.broadcasted_iota(jnp.int32, (CHUNKS, half_w), 1)
    diag = (lane % CHUNKS) == sub
    zero = jnp.zeros((CHUNKS, LANES), BF16)

    def token(t):
        xhi, xlo = _split_bf16(h_ref[t])
        lhs = jnp.concatenate([
            jnp.concatenate([xhi, zero], axis=1), jnp.concatenate([xlo, zero], axis=1),
            jnp.concatenate([zero, xhi], axis=1), jnp.concatenate([zero, xlo], axis=1)], axis=0)
        for tile in range(N_TILES):
            s = _dot_nt(lhs, _tile_weights(tab_ref, off_ref, row0 + t, tile))
            sa = s[0:CHUNKS] + s[CHUNKS:2 * CHUNKS]
            sb = s[2 * CHUNKS:3 * CHUNKS] + s[3 * CHUNKS:4 * CHUNKS]
            ua = jnp.sum(jnp.where(diag, sa, 0.0), axis=0, keepdims=True)
            ub = jnp.sum(jnp.where(diag, sb, 0.0), axis=0, keepdims=True)
            u_scr[pl.ds(t, 1), tile * tile_w:tile * tile_w + half_w] = ua
            u_scr[pl.ds(t, 1), tile * tile_w + half_w:(tile + 1) * tile_w] = ub

    _token_loop(tq, token)
    uhi, ulo = _split_bf16(u_scr[...])
    a = _dot(uhi, g_ref[...]) + _dot(ulo, g_ref[...])
    gelu = 0.5 * a * (1.0 + lax.erf(a * (2.0 ** -0.5)))
    coef_ref[...] = gate_ref[...] * gelu


def _peer2_kernel(off_hbm, coef_ref, x_ref, g2_ref, tab_ref, e_ref, o_ref, c_scr, off_ref, off_sem):
    tq = x_ref.shape[0]
    row0 = _stage_offsets(off_hbm, off_ref, off_sem)
    half_w = MXU_TILE
    tile_w = 2 * half_w
    chi, clo = _split_bf16(coef_ref[...])
    c_scr[...] = _dot(chi, e_ref[...]) + _dot(clo, e_ref[...])
    sub = lax.broadcasted_iota(jnp.int32, (2 * CHUNKS, half_w), 0)
    lane = lax.broadcasted_iota(jnp.int32, (2 * CHUNKS, half_w), 1)
    diag = (lane % CHUNKS) == (sub % CHUNKS)
    group_a = sub < CHUNKS
    g2 = g2_ref[...]

    def token(t):
        acc = jnp.zeros((4 * CHUNKS, 2 * LANES), F32)
        for tile in range(N_TILES):
            ra = c_scr[pl.ds(t, 1), tile * tile_w:tile * tile_w + half_w]
            rb = c_scr[pl.ds(t, 1), tile * tile_w + half_w:(tile + 1) * tile_w]
            both = jnp.where(group_a, jnp.broadcast_to(ra, diag.shape), jnp.broadcast_to(rb, diag.shape))
            lhi, llo = _split_bf16(jnp.where(diag, both, 0.0))
            lhs = jnp.concatenate([lhi, llo], axis=0)
            acc = acc + _dot(lhs, _tile_weights(tab_ref, off_ref, row0 + t, tile))
        y = (acc[0:CHUNKS, 0:LANES] + acc[2 * CHUNKS:3 * CHUNKS, 0:LANES]
             + acc[CHUNKS:2 * CHUNKS, LANES:] + acc[3 * CHUNKS:, LANES:])
        o_ref[t] = x_ref[t] + g2 * y

    _token_loop(tq, token)


def _peer_constants():
    sel = (np.arange(PICK_LANES)[:, None] // CHUNKS) == np.arange(N_PICKS)[None, :]
    g = jnp.asarray(sel.astype(np.float32), dtype=BF16)
    return g, g.T


def _peer_tokens_per_step():
    return 256


def _peer1(off, gate, h3, tab, g, l):
    T = off.shape[0]
    tq = _peer_tokens_per_step()
    row = lambda: pl.BlockSpec((tq, N_PICKS), lambda i: (i, 0))
    return pl.pallas_call(
        _peer1_kernel,
        grid=(T // tq,),
        in_specs=[
            pl.BlockSpec(memory_space=pl.ANY),
            row(),
            pl.BlockSpec((tq, CHUNKS, LANES), lambda i: (i, 0, 0)),
            _resident((None,) + tab.shape[1:], lambda i: (l, 0, 0)),
            _resident((PICK_LANES, N_PICKS), lambda i: (0, 0)),
        ],
        out_specs=row(),
        out_shape=jax.ShapeDtypeStruct((T, N_PICKS), F32),
        scratch_shapes=[pltpu.VMEM((tq, PICK_LANES), F32),
                        pltpu.SMEM((2 * tq, N_PICKS), jnp.int32),
                        pltpu.SemaphoreType.DMA((2,))],
        compiler_params=_cparams(48, ("arbitrary",)),
        name="peer_scores",
    )(off, gate, h3, tab, g)


def _peer2(off, coef, x3, g2, tab, e, l, S):
    T = off.shape[0]
    tq = _peer_tokens_per_step()
    spb = S // tq
    row = lambda: pl.BlockSpec((tq, N_PICKS), lambda i: (i, 0))
    tok = lambda: pl.BlockSpec((tq, CHUNKS, LANES), lambda i: (i, 0, 0))
    return pl.pallas_call(
        _peer2_kernel,
        grid=(T // tq,),
        in_specs=[
            pl.BlockSpec(memory_space=pl.ANY),
            row(), tok(),
            pl.BlockSpec((None, CHUNKS, LANES), lambda i: (i // spb, 0, 0)),
            _resident((None,) + tab.shape[1:], lambda i: (l, 0, 0)),
            _resident((N_PICKS, PICK_LANES), lambda i: (0, 0)),
        ],
        out_specs=tok(),
        out_shape=jax.ShapeDtypeStruct((T, CHUNKS, LANES), F32),
        scratch_shapes=[pltpu.VMEM((tq, PICK_LANES), F32),
                        pltpu.SMEM((2 * tq, N_PICKS), jnp.int32),
                        pltpu.SemaphoreType.DMA((2,))],
        compiler_params=_cparams(48, ("arbitrary",)),
        name="peer_combine",
    )(off, coef, x3, g2, tab, e)


def kernel(x, c, positions, ada_w, ada_b, norm_mix_g, w_in, q_norm_g, k_norm_g, attn_sinks,
           w_o_attn, conv_w, conv_b, conv_ln_g, conv_ln_b, w_o_conv, w_out, norm_ffn_g,
           peer_w_query, peer_sub_keys, peer_u, peer_v):
    B, S, D = x.shape
    depth = ada_w.shape[0]
    T = B * S
    assert D == CHUNKS * LANES and S % 512 == 0

    cos128, sin128 = _rope_tables(positions)
    mod = _modulation(c, ada_w, ada_b)
    mod4 = mod.reshape(depth, B, 1, 6 * D)
    g2_all = mod[:, :, 5 * D:6 * D].reshape(depth, B, CHUNKS, LANES)

    w_in_bf = w_in.astype(BF16)
    w_o_attn_bf = w_o_attn.astype(BF16)
    w_o_conv_bf = w_o_conv.astype(BF16)
    w_out_bf = w_out.astype(BF16)
    w_query_bf = peer_w_query.astype(BF16)
    keys_bf = peer_sub_keys.astype(BF16).reshape(depth, 2 * N_RET_HEADS, N_KEYS, peer_sub_keys.shape[-1])
    u_bf = _pack_table(peer_u)
    v_bf = _pack_table(peer_v)
    pick_sum, pick_spread = _peer_constants()
    vec3 = lambda a: a.reshape(depth, 1, a.shape[-1])
    gq128 = jnp.concatenate([q_norm_g, q_norm_g], axis=-1).reshape(depth, 1, LANES)
    gk128 = jnp.concatenate([k_norm_g, k_norm_g], axis=-1).reshape(depth, 1, LANES)
    norm_mix3, norm_ffn3 = vec3(norm_mix_g), vec3(norm_ffn_g)
    conv_b3, ln_g3, ln_b3 = vec3(conv_b), vec3(conv_ln_g), vec3(conv_ln_b)

    xt = x.reshape(T, D)
    for l in range(depth):
        q, kv, glu, gates = _in_projection(xt, mod4, norm_mix3, w_in_bf, l, S)
        attn_o = _attention(q, kv, cos128, sin128, gq128[l], gk128[l], attn_sinks[l], B, S)
        conv_o = _conv_branch(glu, conv_w, conv_b3, ln_g3, ln_b3, w_o_conv_bf, l, B, S)
        x_mid, h2, off, gate = _merge_route(xt, attn_o, conv_o, gates, w_o_attn_bf, w_out_bf, mod4, norm_ffn3,
                                            w_query_bf, keys_bf, l, S)
        coef = _peer1(off, gate, h2.reshape(T, CHUNKS, LANES), u_bf, pick_sum, l)
        x3 = _peer2(off, coef, x_mid.reshape(T, CHUNKS, LANES), g2_all[l], v_bf, pick_spread, l, S)
        xt = x3.reshape(T, D)
    return xt.reshape(B, S, D)
```

```python
import numpy as np
import jax
import jax.numpy as jnp
from jax import lax
from jax.experimental import pallas as pl
from jax.experimental.pallas import tpu as pltpu

F32 = jnp.float32
BF16 = jnp.bfloat16

HEAD_DIM = 64
N_Q_HEADS = 16
N_KV_HEADS = 4
Q_PER_KV = N_Q_HEADS // N_KV_HEADS
WINDOW = 128
ROPE_THETA = 10000.0
CONV_WIDTH = 31
N_KEYS = 128
N_RET_HEADS = 8
TOPK = 16
N_PICKS = N_RET_HEADS * TOPK
EPS = 1e-6
NEG_INF = -1e30

LANES = 128
SUBLANES = 8
CHUNKS = 8
EXPERT_WORD_ROWS = CHUNKS // 2
MXU_TILE = 256
HALF_TILE = MXU_TILE // CHUNKS
PICKS_PER_TILE = 2 * HALF_TILE
N_TILES = N_PICKS // PICKS_PER_TILE
PICK_LANES = N_PICKS * CHUNKS


def _cparams(vmem_mb, sem):
    return pltpu.CompilerParams(dimension_semantics=sem, vmem_limit_bytes=vmem_mb * 1024 * 1024)


def _resident(block_shape, index_map):
    return pl.BlockSpec(block_shape, index_map, pipeline_mode=pl.Buffered(1))


def _split_bf16(v):
    hi = v.astype(BF16)
    lo = (v - hi.astype(F32)).astype(BF16)
    return hi, lo


def _dot(a, b):
    return jnp.dot(a, b, preferred_element_type=F32)


def _dot_nt(a, b):
    return lax.dot_general(a, b, (((1,), (1,)), ((), ())), preferred_element_type=F32)


def _rope_kernel(pos_ref, inv_ref, cs_ref):
    ang = inv_ref[...] * pos_ref[...].astype(F32)
    cs_ref[0:32, :] = jnp.cos(ang)
    cs_ref[32:64, :] = jnp.sin(ang)


def _rope_tables(positions):
    T = positions.size
    tb = min(2048, T)
    inv = ROPE_THETA ** (-jnp.arange(0, HEAD_DIM, 2, dtype=F32) / HEAD_DIM)
    cs = pl.pallas_call(
        _rope_kernel,
        grid=(T // tb,),
        in_specs=[pl.BlockSpec((1, tb), lambda i: (0, i)), pl.BlockSpec((32, 1), lambda i: (0, 0))],
        out_specs=pl.BlockSpec((64, tb), lambda i: (0, i)),
        out_shape=jax.ShapeDtypeStruct((64, T), F32),
        compiler_params=_cparams(32, ("arbitrary",)),
        name="rope_tables",
    )(positions.reshape(1, T), inv.reshape(32, 1))
    cos = cs[0:32].T
    sin = cs[32:64].T
    cos128 = jnp.concatenate([cos, cos, cos, cos], axis=1)
    sin128 = jnp.concatenate([-sin, sin, -sin, sin], axis=1)
    return cos128, sin128


def _mod_kernel(c_ref, w_ref, b_ref, o_ref):
    c = c_ref[...]
    ca = c * jax.nn.sigmoid(c)
    o_ref[...] = jnp.dot(ca, w_ref[...], preferred_element_type=F32,
                         precision=lax.Precision.HIGHEST) + b_ref[...]


def _modulation(c, ada_w, ada_b):
    depth, D, six_d = ada_w.shape
    B = c.shape[0]
    tn = 1024
    return pl.pallas_call(
        _mod_kernel,
        grid=(depth, six_d // tn),
        in_specs=[
            pl.BlockSpec((B, D), lambda l, j: (0, 0)),
            pl.BlockSpec((None, D, tn), lambda l, j: (l, 0, j)),
            pl.BlockSpec((None, 1, tn), lambda l, j: (l, 0, j)),
        ],
        out_specs=pl.BlockSpec((None, B, tn), lambda l, j: (l, 0, j)),
        out_shape=jax.ShapeDtypeStruct((depth, B, six_d), F32),
        compiler_params=_cparams(32, ("arbitrary", "arbitrary")),
        name="adaln_modulation",
    )(c, ada_w, ada_b.reshape(depth, 1, six_d))


def _in_kernel(x_ref, mod_ref, g_ref, w_ref, q_ref, kv_ref, glu_ref, gates_ref):
    D = x_ref.shape[1]
    x = x_ref[...]
    y = x * lax.rsqrt(jnp.mean(x * x, axis=-1, keepdims=True) + EPS) * g_ref[...]
    h = (y * (1.0 + mod_ref[:, D:2 * D]) + mod_ref[:, 0:D]).astype(BF16)
    c0 = 0
    for ref in (q_ref, kv_ref, glu_ref, gates_ref):
        n = ref.shape[1]
        ref[...] = _dot(h, w_ref[:, c0:c0 + n])
        c0 += n


def _in_projection(xt, mod4, norm_g, w_in_bf, l, S):
    T, D = xt.shape
    n_in = w_in_bf.shape[2]
    tm = 512
    q_w = N_Q_HEADS * HEAD_DIM
    kv_w = 2 * N_KV_HEADS * HEAD_DIM
    rest = (n_in - q_w - kv_w) // 2
    spb = S // tm
    widths = (q_w, kv_w, rest, rest)
    return pl.pallas_call(
        _in_kernel,
        grid=(T // tm,),
        in_specs=[
            pl.BlockSpec((tm, D), lambda i: (i, 0)),
            pl.BlockSpec((None, None, 1, mod4.shape[3]), lambda i: (l, i // spb, 0, 0)),
            pl.BlockSpec((None, 1, D), lambda i: (l, 0, 0)),
            _resident((None, D, n_in), lambda i: (l, 0, 0)),
        ],
        out_specs=[pl.BlockSpec((tm, w), lambda i: (i, 0)) for w in widths],
        out_shape=[jax.ShapeDtypeStruct((T, w), F32) for w in widths],
        compiler_params=_cparams(48, ("parallel",)),
        name="in_projection",
    )(xt, mod4, norm_g, w_in_bf)


def _norm_rope_chunk(raw, g, cos, sin, lane):
    sq = raw * raw
    low = lane < HEAD_DIM
    ss0 = jnp.sum(jnp.where(low, sq, 0.0), axis=-1, keepdims=True)
    ss1 = jnp.sum(jnp.where(low, 0.0, sq), axis=-1, keepdims=True)
    scale = jnp.where(low, lax.rsqrt(ss0 / HEAD_DIM + EPS), lax.rsqrt(ss1 / HEAD_DIM + EPS))
    xg = raw * scale * g
    first = (lane % HEAD_DIM) < (HEAD_DIM // 2)
    rot = jnp.where(first, pltpu.roll(xg, LANES - HEAD_DIM // 2, 1), pltpu.roll(xg, HEAD_DIM // 2, 1))
    return xg * cos + rot * sin


def _attn_kernel(sink_ref, q_ref, kvc_ref, kvp_ref, cosc_ref, sinc_ref, cosp_ref, sinp_ref,
                 gq_ref, gk_ref, o_ref, kd_scr, vd_scr, q_scr, s_scr, p_scr):
    tq = q_ref.shape[0]
    blk = WINDOW
    j = pl.program_id(1)
    kvw = N_KV_HEADS * HEAD_DIM
    gq = gq_ref[...]
    gk = gk_ref[...]

    for (src, cos_r, sin_r, r0, n) in ((kvp_ref, cosp_ref, sinp_ref, 0, blk),
                                       (kvc_ref, cosc_ref, sinc_ref, blk, tq)):
        lane = lax.broadcasted_iota(jnp.int32, (n, LANES), 1)
        low = lane < HEAD_DIM
        cos = cos_r[...]
        sin = sin_r[...]
        for c in range(kvw // LANES):
            kc = _norm_rope_chunk(src[:, c * LANES:(c + 1) * LANES], gk, cos, sin, lane)
            vc = src[:, kvw + c * LANES:kvw + (c + 1) * LANES]
            kc_sw = pltpu.roll(kc, HEAD_DIM, 1)
            vc_sw = pltpu.roll(vc, HEAD_DIM, 1)
            kd_scr[2 * c, r0:r0 + n, :] = jnp.where(low, kc, kc_sw).astype(BF16)
            kd_scr[2 * c + 1, r0:r0 + n, :] = jnp.where(low, kc_sw, kc).astype(BF16)
            vd_scr[2 * c, r0:r0 + n, :] = jnp.where(low, vc, vc_sw).astype(BF16)
            vd_scr[2 * c + 1, r0:r0 + n, :] = jnp.where(low, vc_sw, vc).astype(BF16)

    scale = HEAD_DIM ** -0.5
    lane_t = lax.broadcasted_iota(jnp.int32, (tq, LANES), 1)
    low_t = lane_t < HEAD_DIM
    for c in range(N_Q_HEADS // 2):
        qc = _norm_rope_chunk(q_ref[:, c * LANES:(c + 1) * LANES], gq, cosc_ref[...], sinc_ref[...],
                              lane_t) * scale
        q_scr[2 * c] = jnp.where(low_t, qc, 0.0).astype(BF16)
        q_scr[2 * c + 1] = jnp.where(low_t, 0.0, qc).astype(BF16)

    low_q = lax.broadcasted_iota(jnp.int32, (blk, LANES), 1) < HEAD_DIM
    qi = lax.broadcasted_iota(jnp.int32, (blk, 2 * blk), 0)
    kj = lax.broadcasted_iota(jnp.int32, (blk, 2 * blk), 1)
    diff = qi + blk - kj
    in_window = (diff >= 0) & (diff < WINDOW)

    for n in range(tq // blk):
        rows = slice(n * blk, (n + 1) * blk)
        keys = slice(n * blk, (n + 2) * blk)
        valid = in_window
        if n == 0:
            valid = valid & (kj >= jnp.where(j > 0, 0, blk))
        for g in range(N_KV_HEADS):
            q4 = jnp.concatenate([q_scr[g * Q_PER_KV + r, rows, :] for r in range(Q_PER_KV)], axis=0)
            s_scr[g] = _dot_nt(q4, kd_scr[g, keys, :])
        for g in range(N_KV_HEADS):
            for r in range(Q_PER_KV):
                hrows = slice(r * blk, (r + 1) * blk)
                s = jnp.where(valid, s_scr[g, hrows, :], NEG_INF)
                sink = sink_ref[g * Q_PER_KV + r]
                m = jnp.maximum(jnp.max(s, axis=-1, keepdims=True), sink)
                p = jnp.exp(s - m)
                p = p / (jnp.sum(p, axis=-1, keepdims=True) + jnp.exp(sink - m))
                p_scr[g, hrows, :] = p.astype(BF16)
        for g in range(N_KV_HEADS):
            o4 = _dot(p_scr[g], vd_scr[g, keys, :])
            for r2 in range(Q_PER_KV // 2):
                c = (g * Q_PER_KV) // 2 + r2
                oe = o4[(2 * r2) * blk:(2 * r2 + 1) * blk, :]
                oo = o4[(2 * r2 + 1) * blk:(2 * r2 + 2) * blk, :]
                o_ref[rows, c * LANES:(c + 1) * LANES] = jnp.where(low_q, oe, oo).astype(BF16)


def _attention(q, kv, cos128, sin128, gq, gk, sinks, B, S):
    T = q.shape[0]
    tq = 512
    blk = WINDOW
    nq = S // tq
    bpq = tq // blk
    cur = lambda b, j: (b * nq + j, 0)
    prev = lambda b, j: (jnp.maximum((b * nq + j) * bpq - 1, 0), 0)
    kvw = kv.shape[1]
    return pl.pallas_call(
        _attn_kernel,
        grid=(B, nq),
        in_specs=[
            pl.BlockSpec(memory_space=pltpu.SMEM),
            pl.BlockSpec((tq, q.shape[1]), cur),
            pl.BlockSpec((tq, kvw), cur),
            pl.BlockSpec((blk, kvw), prev),
            pl.BlockSpec((tq, LANES), cur),
            pl.BlockSpec((tq, LANES), cur),
            pl.BlockSpec((blk, LANES), prev),
            pl.BlockSpec((blk, LANES), prev),
            pl.BlockSpec((1, LANES), lambda b, j: (0, 0)),
            pl.BlockSpec((1, LANES), lambda b, j: (0, 0)),
        ],
        out_specs=pl.BlockSpec((tq, q.shape[1]), cur),
        scratch_shapes=[pltpu.VMEM((N_KV_HEADS, tq + blk, LANES), BF16),
                        pltpu.VMEM((N_KV_HEADS, tq + blk, LANES), BF16),
                        pltpu.VMEM((N_Q_HEADS, tq, LANES), BF16),
                        pltpu.VMEM((N_KV_HEADS, Q_PER_KV * blk, 2 * blk), F32),
                        pltpu.VMEM((N_KV_HEADS, Q_PER_KV * blk, 2 * blk), BF16)],
        out_shape=jax.ShapeDtypeStruct((T, q.shape[1]), BF16),
        compiler_params=_cparams(48, ("arbitrary", "arbitrary")),
        name="swa_attention",
    )(sinks, q, kv, kv, cos128, sin128, cos128, sin128, gq, gk)


HALO = 32


def _conv_kernel(glu_ref, cw_ref, cb_ref, lg_ref, lb_ref, wo_ref, o_ref, ubuf):
    tc = glu_ref.shape[0]
    C = o_ref.shape[1]
    j = pl.program_id(1)

    @pl.when(j == 0)
    def _():
        ubuf[0:HALO, :] = jnp.zeros((HALO, C), F32)
        ubuf[HALO + tc:, :] = jnp.zeros((SUBLANES, C), F32)

    @pl.when(j > 0)
    def _():
        ubuf[0:HALO, :] = ubuf[tc:tc + HALO, :]

    ubuf[HALO:HALO + tc, :] = glu_ref[:, 0:C] * jax.nn.sigmoid(glu_ref[:, C:2 * C])
    off = HALO - (CONV_WIDTH - 1)
    acc = cb_ref[...]
    for rem in range(SUBLANES):
        part = None
        for k in range(CONV_WIDTH):
            if (off + k) % SUBLANES == rem:
                base = off + k - rem
                term = cw_ref[k:k + 1, :] * ubuf[base:base + tc + SUBLANES, :]
                part = term if part is None else part + term
        acc = acc + part[rem:rem + tc, :]
    mu = jnp.mean(acc, axis=-1, keepdims=True)
    d = acc - mu
    var = jnp.mean(d * d, axis=-1, keepdims=True)
    yn = d * lax.rsqrt(var + EPS) * lg_ref[...] + lb_ref[...]
    act = yn * jax.nn.sigmoid(yn)
    o_ref[...] = _dot(act.astype(BF16), wo_ref[...])


def _conv_branch(glu, conv_w, conv_b, ln_g, ln_b, w_o_conv_bf, l, B, S):
    T = glu.shape[0]
    C = glu.shape[1] // 2
    tc = 256
    nc = S // tc
    vec = lambda: pl.BlockSpec((None, 1, C), lambda b, j: (l, 0, 0))
    return pl.pallas_call(
        _conv_kernel,
        grid=(B, nc),
        in_specs=[
            pl.BlockSpec((tc, 2 * C), lambda b, j: (b * nc + j, 0)),
            pl.BlockSpec((None, CONV_WIDTH, C), lambda b, j: (l, 0, 0)),
            vec(), vec(), vec(),
            _resident((None, C, C), lambda b, j: (l, 0, 0)),
        ],
        out_specs=pl.BlockSpec((tc, C), lambda b, j: (b * nc + j, 0)),
        out_shape=jax.ShapeDtypeStruct((T, C), F32),
        scratch_shapes=[pltpu.VMEM((tc + HALO + SUBLANES, C), F32)],
        compiler_params=_cparams(48, ("arbitrary", "arbitrary")),
        name="conv_branch",
    )(glu, conv_w, conv_b, ln_g, ln_b, w_o_conv_bf)


def _merge_route_kernel(x_ref, ao_ref, co_ref, gates_ref, woa_ref, wout_ref, mod_ref, gf_ref,
                        wq_ref, keys_ref, xmid_ref, h2_ref, off_ref, gate_ref):
    D = x_ref.shape[1]
    attn = _dot(ao_ref[...], woa_ref[...])
    merged = (jax.nn.sigmoid(gates_ref[:, 0:D]) * attn
              + jax.nn.sigmoid(gates_ref[:, D:2 * D]) * co_ref[...])
    out = _dot(merged.astype(BF16), wout_ref[...])
    xm = x_ref[...] + mod_ref[:, 2 * D:3 * D] * out
    xmid_ref[...] = xm
    y = xm * lax.rsqrt(jnp.mean(xm * xm, axis=-1, keepdims=True) + EPS) * gf_ref[...]
    h2 = y * (1.0 + mod_ref[:, 4 * D:5 * D]) + mod_ref[:, 3 * D:4 * D]
    h2_ref[...] = h2
    _route_body(h2, wq_ref, keys_ref, off_ref, gate_ref)


def _merge_route(xt, attn_o, conv_o, gates, w_o_attn_bf, w_out_bf, mod4, norm_ffn_g, w_query_bf, keys_bf, l, S):
    T, D = xt.shape
    tm = 256
    spb = S // tm
    nq = w_query_bf.shape[2]
    row = lambda w: pl.BlockSpec((tm, w), lambda i: (i, 0))
    return pl.pallas_call(
        _merge_route_kernel,
        grid=(T // tm,),
        in_specs=[
            row(D), row(D), row(D), row(2 * D),
            _resident((None, D, D), lambda i: (l, 0, 0)),
            _resident((None, D, D), lambda i: (l, 0, 0)),
            pl.BlockSpec((None, None, 1, mod4.shape[3]), lambda i: (l, i // spb, 0, 0)),
            pl.BlockSpec((None, 1, D), lambda i: (l, 0, 0)),
            _resident((None, D, nq), lambda i: (l, 0, 0)),
            _resident((None, 2 * N_RET_HEADS, N_KEYS, N_KEYS), lambda i: (l, 0, 0, 0)),
        ],
        out_specs=[row(D), row(D), row(N_PICKS), row(N_PICKS)],
        out_shape=[jax.ShapeDtypeStruct((T, D), F32), jax.ShapeDtypeStruct((T, D), F32),
                   jax.ShapeDtypeStruct((T, N_PICKS), jnp.int32), jax.ShapeDtypeStruct((T, N_PICKS), F32)],
        compiler_params=_cparams(48, ("parallel",)),
        name="merge_projection_and_route",
    )(xt, attn_o, conv_o, gates, w_o_attn_bf, w_out_bf, mod4, norm_ffn_g, w_query_bf, keys_bf)


def _topk_rows(s, payload, k):
    R, n = s.shape
    rid = lax.broadcasted_iota(jnp.int32, (R, n), 0).astype(F32)
    vals, pays = [], []
    for _ in range(k):
        m = jnp.max(s, axis=0, keepdims=True)
        pos = jnp.min(jnp.where(s == m, rid, float(R)), axis=0, keepdims=True)
        sel = rid == pos
        if payload is None:
            pays.append(pos)
        else:
            pays.append(jnp.max(jnp.where(sel, payload, -1.0), axis=0, keepdims=True))
        vals.append(m)
        s = jnp.where(sel, -jnp.inf, s)
    return jnp.concatenate(vals, axis=0), jnp.concatenate(pays, axis=0)


_CANDS = [(a, b) for a in range(TOPK) for b in range(TOPK) if (a + 1) * (b + 1) <= TOPK]


def _cand_rows(v0, v1):
    n_pad = -len(_CANDS) % SUBLANES
    ra = jnp.concatenate([v0[a:a + 1, :] for a, _ in _CANDS] + [v0[0:1, :]] * n_pad, axis=0)
    rb = jnp.concatenate([v1[b:b + 1, :] for _, b in _CANDS] + [v1[0:1, :]] * n_pad, axis=0)
    row = lax.broadcasted_iota(jnp.int32, ra.shape, 0)
    return ra, rb, row < len(_CANDS)


def _route_body(h, wq_ref, keys_ref, off_ref, gate_ref):
    q = _dot(h.astype(BF16), wq_ref[...])
    ids_all, gates_all = [], []
    for h in range(N_RET_HEADS):
        half = []
        for p in range(2):
            hp = 2 * h + p
            qs = q[:, hp * N_KEYS:(hp + 1) * N_KEYS].astype(BF16)
            st = _dot_nt(keys_ref[hp], qs)
            half.append(_topk_rows(st, None, TOPK))
        (v0, i0), (v1, i1) = half
        sa, sb, real = _cand_rows(v0, v1)
        ia, ib, _ = _cand_rows(i0, i1)
        cand_s = jnp.where(real, sa + sb, -jnp.inf)
        cand_i = ia * float(N_KEYS) + ib
        top_s, top_i = _topk_rows(cand_s, cand_i, TOPK)
        e = jnp.exp(top_s - top_s[0:1, :])
        gates_all.append(e / jnp.sum(e, axis=0, keepdims=True))
        ids_all.append(top_i)
    ids = jnp.concatenate(ids_all, axis=0).T.astype(jnp.int32)
    gate_ref[...] = jnp.concatenate(gates_all, axis=0).T
    off_ref[...] = ids * EXPERT_WORD_ROWS


TOKENS_PER_ITER = 32


def _bf16_bits(v):
    b = pltpu.bitcast(v, jnp.uint32)
    return (b + jnp.uint32(0x7FFF) + ((b >> 16) & jnp.uint32(1))) >> 16


def _pack_kernel(x_ref, o_ref):
    n_rows = x_ref.shape[0]
    words_per_expert = CHUNKS // 2
    for j in range(words_per_expert):
        lo = _bf16_bits(x_ref[:, (2 * j) * LANES:(2 * j + 1) * LANES])
        hi = _bf16_bits(x_ref[:, (2 * j + 1) * LANES:(2 * j + 2) * LANES])
        o_ref[pl.ds(j, n_rows, stride=words_per_expert), :] = lo | (hi << 16)


def _pack_table(tab):
    depth, n_exp, d = tab.shape
    rows_per_step = 512
    out_rows = rows_per_step * d // (2 * LANES)
    return pl.pallas_call(
        _pack_kernel,
        grid=(depth, n_exp // rows_per_step),
        in_specs=[pl.BlockSpec((None, rows_per_step, d), lambda l, i: (l, i, 0))],
        out_specs=pl.BlockSpec((None, out_rows, LANES), lambda l, i: (l, i, 0)),
        out_shape=jax.ShapeDtypeStruct((depth, n_exp * d // (2 * LANES), LANES), jnp.uint32),
        compiler_params=_cparams(32, ("arbitrary", "arbitrary")),
        name="pack_expert_table",
    )(tab)


def _expert_rows(tab_ref, off_ref, t, first, count):
    words = jnp.concatenate(
        [tab_ref[pl.ds(pl.multiple_of(off_ref.at[:, pl.ds(first + i, 1)][t, 0], EXPERT_WORD_ROWS),
                       EXPERT_WORD_ROWS), :]
         for i in range(count)], axis=0)
    return pltpu.bitcast(words, BF16)


def _tile_weights(tab_ref, off_ref, t, tile):
    base = tile * PICKS_PER_TILE
    wa = _expert_rows(tab_ref, off_ref, t, base, HALF_TILE)
    wb = _expert_rows(tab_ref, off_ref, t, base + HALF_TILE, HALF_TILE)
    return jnp.concatenate([wa, wb], axis=1)


def _stage_offsets(off_hbm, off_smem, sem):
    tq = off_smem.shape[0] // 2
    i = pl.program_id(0)
    slot = lax.rem(i, 2)

    def copy(step, s):
        return pltpu.make_async_copy(off_hbm.at[pl.ds(step * tq, tq), :],
                                     off_smem.at[pl.ds(s * tq, tq), :], sem.at[s])

    @pl.when(i == 0)
    def _():
        copy(0, 0).start()

    @pl.when(i + 1 < pl.num_programs(0))
    def _():
        copy(i + 1, 1 - slot).start()

    copy(i, slot).wait()
    return slot * tq


def _token_loop(n_tokens, token):
    def body(i, carry):
        for r in range(TOKENS_PER_ITER):
            token(i * TOKENS_PER_ITER + r)
        return carry

    lax.fori_loop(0, n_tokens // TOKENS_PER_ITER, body, 0)


def _peer1_kernel(off_hbm, gate_ref, h_ref, tab_ref, g_ref, coef_ref, u_scr, off_ref, off_sem):
    tq = h_ref.shape[0]
    row0 = _stage_offsets(off_hbm, off_ref, off_sem)
    half_w = MXU_TILE
    tile_w = 2 * half_w
    sub = lax.broadcasted_iota(jnp.int32, (CHUNKS, half_w), 0)
    lane = lax.broadcasted_iota(jnp.int32, (CHUNKS, half_w), 1)
    diag = (lane % CHUNKS) == sub
    zero = jnp.zeros((CHUNKS, LANES), BF16)

    def token(t):
        xhi, xlo = _split_bf16(h_ref[t])
        lhs = jnp.concatenate([
            jnp.concatenate([xhi, zero], axis=1), jnp.concatenate([xlo, zero], axis=1),
            jnp.concatenate([zero, xhi], axis=1), jnp.concatenate([zero, xlo], axis=1)], axis=0)
        for tile in range(N_TILES):
            s = _dot_nt(lhs, _tile_weights(tab_ref, off_ref, row0 + t, tile))
            sa = s[0:CHUNKS] + s[CHUNKS:2 * CHUNKS]
            sb = s[2 * CHUNKS:3 * CHUNKS] + s[3 * CHUNKS:4 * CHUNKS]
            ua = jnp.sum(jnp.where(diag, sa, 0.0), axis=0, keepdims=True)
            ub = jnp.sum(jnp.where(diag, sb, 0.0), axis=0, keepdims=True)
            u_scr[pl.ds(t, 1), tile * tile_w:tile * tile_w + half_w] = ua
            u_scr[pl.ds(t, 1), tile * tile_w + half_w:(tile + 1) * tile_w] = ub

    _token_loop(tq, token)
    uhi, ulo = _split_bf16(u_scr[...])
    a = _dot(uhi, g_ref[...]) + _dot(ulo, g_ref[...])
    gelu = 0.5 * a * (1.0 + lax.erf(a * (2.0 ** -0.5)))
    coef_ref[...] = gate_ref[...] * gelu


def _peer2_kernel(off_hbm, coef_ref, x_ref, g2_ref, tab_ref, e_ref, o_ref, c_scr, off_ref, off_sem):
    tq = x_ref.shape[0]
    row0 = _stage_offsets(off_hbm, off_ref, off_sem)
    half_w = MXU_TILE
    tile_w = 2 * half_w
    chi, clo = _split_bf16(coef_ref[...])
    c_scr[...] = _dot(chi, e_ref[...]) + _dot(clo, e_ref[...])
    sub = lax.broadcasted_iota(jnp.int32, (2 * CHUNKS, half_w), 0)
    lane = lax.broadcasted_iota(jnp.int32, (2 * CHUNKS, half_w), 1)
    diag = (lane % CHUNKS) == (sub % CHUNKS)
    group_a = sub < CHUNKS
    g2 = g2_ref[...]

    def token(t):
        acc = jnp.zeros((4 * CHUNKS, 2 * LANES), F32)
        for tile in range(N_TILES):
            ra = c_scr[pl.ds(t, 1), tile * tile_w:tile * tile_w + half_w]
            rb = c_scr[pl.ds(t, 1), tile * tile_w + half_w:(tile + 1) * tile_w]
            both = jnp.where(group_a, jnp.broadcast_to(ra, diag.shape), jnp.broadcast_to(rb, diag.shape))
            lhi, llo = _split_bf16(jnp.where(diag, both, 0.0))
            lhs = jnp.concatenate([lhi, llo], axis=0)
            acc = acc + _dot(lhs, _tile_weights(tab_ref, off_ref, row0 + t, tile))
        y = (acc[0:CHUNKS, 0:LANES] + acc[2 * CHUNKS:3 * CHUNKS, 0:LANES]
             + acc[CHUNKS:2 * CHUNKS, LANES:] + acc[3 * CHUNKS:, LANES:])
        o_ref[t] = x_ref[t] + g2 * y

    _token_loop(tq, token)


def _peer_constants():
    sel = (np.arange(PICK_LANES)[:, None] // CHUNKS) == np.arange(N_PICKS)[None, :]
    g = jnp.asarray(sel.astype(np.float32), dtype=BF16)
    return g, g.T


def _peer_tokens_per_step():
    return 256


def _peer1(off, gate, h3, tab, g, l):
    T = off.shape[0]
    tq = _peer_tokens_per_step()
    row = lambda: pl.BlockSpec((tq, N_PICKS), lambda i: (i, 0))
    return pl.pallas_call(
        _peer1_kernel,
        grid=(T // tq,),
        in_specs=[
            pl.BlockSpec(memory_space=pl.ANY),
            row(),
            pl.BlockSpec((tq, CHUNKS, LANES), lambda i: (i, 0, 0)),
            _resident((None,) + tab.shape[1:], lambda i: (l, 0, 0)),
            _resident((PICK_LANES, N_PICKS), lambda i: (0, 0)),
        ],
        out_specs=row(),
        out_shape=jax.ShapeDtypeStruct((T, N_PICKS), F32),
        scratch_shapes=[pltpu.VMEM((tq, PICK_LANES), F32),
                        pltpu.SMEM((2 * tq, N_PICKS), jnp.int32),
                        pltpu.SemaphoreType.DMA((2,))],
        compiler_params=_cparams(48, ("arbitrary",)),
        name="peer_scores",
    )(off, gate, h3, tab, g)


def _peer2(off, coef, x3, g2, tab, e, l, S):
    T = off.shape[0]
    tq = _peer_tokens_per_step()
    spb = S // tq
    row = lambda: pl.BlockSpec((tq, N_PICKS), lambda i: (i, 0))
    tok = lambda: pl.BlockSpec((tq, CHUNKS, LANES), lambda i: (i, 0, 0))
    return pl.pallas_call(
        _peer2_kernel,
        grid=(T // tq,),
        in_specs=[
            pl.BlockSpec(memory_space=pl.ANY),
            row(), tok(),
            pl.BlockSpec((None, CHUNKS, LANES), lambda i: (i // spb, 0, 0)),
            _resident((None,) + tab.shape[1:], lambda i: (l, 0, 0)),
            _resident((N_PICKS, PICK_LANES), lambda i: (0, 0)),
        ],
        out_specs=tok(),
        out_shape=jax.ShapeDtypeStruct((T, CHUNKS, LANES), F32),
        scratch_shapes=[pltpu.VMEM((tq, PICK_LANES), F32),
                        pltpu.SMEM((2 * tq, N_PICKS), jnp.int32),
                        pltpu.SemaphoreType.DMA((2,))],
        compiler_params=_cparams(48, ("arbitrary",)),
        name="peer_combine",
    )(off, coef, x3, g2, tab, e)


def kernel(x, c, positions, ada_w, ada_b, norm_mix_g, w_in, q_norm_g, k_norm_g, attn_sinks,
           w_o_attn, conv_w, conv_b, conv_ln_g, conv_ln_b, w_o_conv, w_out, norm_ffn_g,
           peer_w_query, peer_sub_keys, peer_u, peer_v):
    B, S, D = x.shape
    depth = ada_w.shape[0]
    T = B * S
    assert D == CHUNKS * LANES and S % 512 == 0

    cos128, sin128 = _rope_tables(positions)
    mod = _modulation(c, ada_w, ada_b)
    mod4 = mod.reshape(depth, B, 1, 6 * D)
    g2_all = mod[:, :, 5 * D:6 * D].reshape(depth, B, CHUNKS, LANES)

    w_in_bf = w_in.astype(BF16)
    w_o_attn_bf = w_o_attn.astype(BF16)
    w_o_conv_bf = w_o_conv.astype(BF16)
    w_out_bf = w_out.astype(BF16)
    w_query_bf = peer_w_query.astype(BF16)
    keys_bf = peer_sub_keys.astype(BF16).reshape(depth, 2 * N_RET_HEADS, N_KEYS, peer_sub_keys.shape[-1])
    u_bf = _pack_table(peer_u)
    v_bf = _pack_table(peer_v)
    pick_sum, pick_spread = _peer_constants()
    vec3 = lambda a: a.reshape(depth, 1, a.shape[-1])
    gq128 = jnp.concatenate([q_norm_g, q_norm_g], axis=-1).reshape(depth, 1, LANES)
    gk128 = jnp.concatenate([k_norm_g, k_norm_g], axis=-1).reshape(depth, 1, LANES)
    norm_mix3, norm_ffn3 = vec3(norm_mix_g), vec3(norm_ffn_g)
    conv_b3, ln_g3, ln_b3 = vec3(conv_b), vec3(conv_ln_g), vec3(conv_ln_b)

    xt = x.reshape(T, D)
    for l in range(depth):
        q, kv, glu, gates = _in_projection(xt, mod4, norm_mix3, w_in_bf, l, S)
        attn_o = _attention(q, kv, cos128, sin128, gq128[l], gk128[l], attn_sinks[l], B, S)
        conv_o = _conv_branch(glu, conv_w, conv_b3, ln_g3, ln_b3, w_o_conv_bf, l, B, S)
        x_mid, h2, off, gate = _merge_route(xt, attn_o, conv_o, gates, w_o_attn_bf, w_out_bf, mod4, norm_ffn3,
                                            w_query_bf, keys_bf, l, S)
        coef = _peer1(off, gate, h2.reshape(T, CHUNKS, LANES), u_bf, pick_sum, l)
        x3 = _peer2(off, coef, x_mid.reshape(T, CHUNKS, LANES), g2_all[l], v_bf, pick_spread, l, S)
        xt = x3.reshape(T, D)
    return xt.reshape(B, S, D)
```
